```python
import math
import jax, jax.numpy as jnp
from jax import lax
import numpy as np

D_MODEL = 4096
BATCH = 2
SEQ = 4096
DEPTH = 2

EXPAND = 2
MIX_WIDTH = EXPAND * D_MODEL
S5_WIDTH = MIX_WIDTH // 4
S5_GROUP = 16
S5_GROUPS = S5_WIDTH // S5_GROUP
S5_STATE = 64
S5_EIG_CLIP = -1e-4
SSD_WIDTH = MIX_WIDTH - S5_WIDTH
SSD_HEAD_DIM = 64
SSD_HEADS = SSD_WIDTH // SSD_HEAD_DIM
SSD_GROUPS = 8
SSD_STATE = 128
SSD_CONV = 4
SSD_CHUNK = 128
SSD_XBC = SSD_WIDTH + 2 * SSD_GROUPS * SSD_STATE
FOX_HEAD_DIM = 128
FOX_HEADS = D_MODEL // FOX_HEAD_DIM
FOX_WIDTH = FOX_HEADS * FOX_HEAD_DIM
FOX_BLOCK = 128
NORM_EPS = 1e-5

EVEN_IN = 2 * S5_WIDTH + SSD_WIDTH + SSD_XBC + SSD_HEADS
ODD_IN = 4 * FOX_WIDTH + FOX_HEADS

kernel_name = "hybrid_s5_ssd_fox_trunk"

F32 = jnp.float32


def rms_norm(x, w):
    xf = x.astype(F32)
    y = xf * lax.rsqrt(jnp.mean(xf * xf, axis=-1, keepdims=True) + NORM_EPS)
    return (y * w.astype(F32)).astype(x.dtype)


def causal_depthwise_conv(x, w, b):
    k_width, ch = w.shape
    y = lax.conv_general_dilated(
        x, w.astype(F32)[:, None, :], window_strides=(1,),
        padding=((k_width - 1, 0),), dimension_numbers=("NWC", "WIO", "NWC"),
        feature_group_count=ch)
    return y + b.astype(F32)


def s5_mixer(u, lam_re, lam_im, log_step, b_re, b_im, c_re, c_im, d, w_glu, b_glu):
    bsz, seqlen, _ = u.shape
    u = u.reshape(bsz, seqlen, S5_GROUPS, S5_GROUP)
    lr = jnp.minimum(lam_re.astype(F32), S5_EIG_CLIP)
    li = lam_im.astype(F32)
    step = jnp.exp(log_step.astype(F32))[:, None]
    mag = jnp.exp(lr * step)
    ab_re = mag * jnp.cos(li * step)
    ab_im = mag * jnp.sin(li * step)
    denom = lr * lr + li * li
    nr = ab_re - 1.0
    ni = ab_im
    coef_re = (nr * lr + ni * li) / denom
    coef_im = (ni * lr - nr * li) / denom
    br = b_re.astype(F32)
    bi = b_im.astype(F32)
    bb_re = coef_re[..., None] * br - coef_im[..., None] * bi
    bb_im = coef_re[..., None] * bi + coef_im[..., None] * br
    bu_re = jnp.einsum('blgh,gph->blgp', u, bb_re)
    bu_im = jnp.einsum('blgh,gph->blgp', u, bb_im)
    a_re = jnp.broadcast_to(ab_re, bu_re.shape)
    a_im = jnp.broadcast_to(ab_im, bu_im.shape)

    def combine(e_i, e_j):
        ar_i, ai_i, br_i, bi_i = e_i
        ar_j, ai_j, br_j, bi_j = e_j
        return (ar_j * ar_i - ai_j * ai_i,
                ar_j * ai_i + ai_j * ar_i,
                ar_j * br_i - ai_j * bi_i + br_j,
                ar_j * bi_i + ai_j * br_i + bi_j)

    _, _, s_re, s_im = lax.associative_scan(combine, (a_re, a_im, bu_re, bu_im), axis=1)
    y = (jnp.einsum('blgp,ghp->blgh', s_re, c_re.astype(F32))
         - jnp.einsum('blgp,ghp->blgh', s_im, c_im.astype(F32))
         + d.astype(F32) * u)
    y = y.reshape(bsz, seqlen, S5_WIDTH)
    g = jax.nn.gelu(y)
    return g * jax.nn.sigmoid(g @ w_glu.astype(F32) + b_glu.astype(F32))


def ssd_chunked(x, dt, a_head, bm, cm):
    bsz, seqlen, nh, hd = x.shape
    ng, ns = bm.shape[2], bm.shape[3]
    r = nh // ng
    q = SSD_CHUNK
    nc = seqlen // q
    xc = (x * dt[..., None]).reshape(bsz, nc, q, ng, r, hd)
    la = (dt * a_head).reshape(bsz, nc, q, ng, r).transpose(0, 3, 4, 1, 2)
    la_cum = jnp.cumsum(la, axis=-1)
    bc = bm.reshape(bsz, nc, q, ng, ns)
    cc = cm.reshape(bsz, nc, q, ng, ns)
    causal = jnp.tril(jnp.ones((q, q), dtype=bool))
    seg = la_cum[..., :, None] - la_cum[..., None, :]
    decay_in = jnp.exp(jnp.where(causal, seg, -jnp.inf))
    scores = jnp.einsum('bcqgn,bckgn->bgcqk', cc, bc)
    w_in = scores[:, :, None] * decay_in
    y_diag = jnp.einsum('bgrcqk,bckgrp->bcqgrp', w_in, xc)
    decay_end = jnp.exp(la_cum[..., -1:] - la_cum).transpose(0, 3, 4, 1, 2)
    states = jnp.einsum('bckgn,bckgrp->bcgrpn', bc, xc * decay_end[..., None])
    chunk_decay = la_cum[..., -1]
    cs = jnp.cumsum(jnp.pad(chunk_decay, ((0, 0), (0, 0), (0, 0), (1, 0))), axis=-1)
    seg_c = cs[..., :, None] - cs[..., None, :]
    mask_c = jnp.tril(jnp.ones((nc + 1, nc + 1), dtype=bool))
    decay_c = jnp.exp(jnp.where(mask_c, seg_c, -jnp.inf))
    states_cat = jnp.concatenate([jnp.zeros_like(states[:, :1]), states], axis=1)
    states_in = jnp.einsum('bgrzc,bcgrpn->bzgrpn', decay_c[..., :nc, :], states_cat)
    decay_out = jnp.exp(la_cum).transpose(0, 3, 4, 1, 2)
    y_off = jnp.einsum('bcqgn,bcgrpn->bcqgrp', cc, states_in) * decay_out[..., None]
    return (y_diag + y_off).reshape(bsz, seqlen, nh, hd)


def ssd_mixer(z, xbc, dt_raw, conv_w, conv_b, dt_bias, a_log, d, norm_w):
    bsz, seqlen, _ = z.shape
    xbc = jax.nn.silu(causal_depthwise_conv(xbc, conv_w, conv_b))
    xs, bm, cm = jnp.split(xbc, [SSD_WIDTH, SSD_WIDTH + SSD_GROUPS * SSD_STATE], axis=-1)
    xs = xs.reshape(bsz, seqlen, SSD_HEADS, SSD_HEAD_DIM)
    bm = bm.reshape(bsz, seqlen, SSD_GROUPS, SSD_STATE)
    cm = cm.reshape(bsz, seqlen, SSD_GROUPS, SSD_STATE)
    dt = jax.nn.softplus(dt_raw + dt_bias.astype(F32))
    a_head = -jnp.exp(a_log.astype(F32))
    y = ssd_chunked(xs, dt, a_head, bm, cm) + d.astype(F32)[:, None] * xs
    y = y.reshape(bsz, seqlen, SSD_WIDTH) * jax.nn.silu(z)
    yg = y.reshape(bsz, seqlen, SSD_GROUPS, SSD_WIDTH // SSD_GROUPS)
    yg = yg * lax.rsqrt(jnp.mean(yg * yg, axis=-1, keepdims=True) + NORM_EPS)
    return yg.reshape(bsz, seqlen, SSD_WIDTH) * norm_w.astype(F32)


def fox_attention(q, k, v, f_logit, b_f):
    bsz, seqlen, nh, hd = q.shape
    log_f = jax.nn.log_sigmoid(f_logit + b_f.astype(F32))
    c = jnp.cumsum(log_f, axis=1).transpose(0, 2, 1)
    nb = seqlen // FOX_BLOCK
    qb = q.reshape(bsz, nb, FOX_BLOCK, nh, hd).transpose(1, 0, 2, 3, 4)
    cb = c.reshape(bsz, nh, nb, FOX_BLOCK).transpose(2, 0, 1, 3)
    kpos = jnp.arange(seqlen)
    scale = 1.0 / math.sqrt(FOX_HEAD_DIM)

    def block(args):
        qi, ci, i = args
        s = jnp.einsum('bqhd,bkhd->bhqk', qi, k) * scale + (ci[..., :, None] - c[:, :, None, :])
        qpos = i * FOX_BLOCK + jnp.arange(FOX_BLOCK)
        s = jnp.where(kpos[None, :] <= qpos[:, None], s, -jnp.inf)
        p = jax.nn.softmax(s, axis=-1)
        return jnp.einsum('bhqk,bkhd->bqhd', p, v)

    out = lax.map(block, (qb, cb, jnp.arange(nb)))
    return out.transpose(1, 0, 2, 3, 4).reshape(bsz, seqlen, nh * hd)


def ssm_layer(x, norm_w, w_in, lam_re, lam_im, log_step, b_re, b_im, c_re, c_im, s5_d,
              w_glu, b_glu, conv_w, conv_b, dt_bias, a_log, ssd_d, ssd_norm_w, w_out):
    h = rms_norm(x, norm_w)
    proj = (h @ w_in).astype(F32)
    s5_u, s5_gate, ssd_z, ssd_xbc, ssd_dt = jnp.split(
        proj, [S5_WIDTH, 2 * S5_WIDTH, 2 * S5_WIDTH + SSD_WIDTH,
               2 * S5_WIDTH + SSD_WIDTH + SSD_XBC], axis=-1)
    s5_out = s5_mixer(s5_u, lam_re, lam_im, log_step, b_re, b_im, c_re, c_im, s5_d,
                      w_glu, b_glu) * jax.nn.silu(s5_gate)
    ssd_out = ssd_mixer(ssd_z, ssd_xbc, ssd_dt, conv_w, conv_b, dt_bias, a_log, ssd_d, ssd_norm_w)
    mixed = jnp.concatenate([s5_out, ssd_out], axis=-1).astype(x.dtype)
    return mixed @ w_out


def fox_layer(x, norm_w, w_in, b_f, w_out):
    bsz, seqlen, _ = x.shape
    h = rms_norm(x, norm_w)
    proj = (h @ w_in).astype(F32)
    q, k, v, gate, f_logit = jnp.split(
        proj, [FOX_WIDTH, 2 * FOX_WIDTH, 3 * FOX_WIDTH, 4 * FOX_WIDTH], axis=-1)
    shp = (bsz, seqlen, FOX_HEADS, FOX_HEAD_DIM)
    att = fox_attention(q.reshape(shp), k.reshape(shp), v.reshape(shp), f_logit, b_f)
    out = (att * jax.nn.silu(gate)).astype(x.dtype)
    return out @ w_out


def setup_inputs(seed: int = 0) -> dict:
    key = jax.random.key(seed)
    ks = jax.random.split(key, 32)
    nrm = lambda k, shp, s: jax.random.normal(k, shp, F32) * s
    x = nrm(ks[0], (BATCH, SEQ, D_MODEL), 1.0)
    l0_norm_w = 1.0 + nrm(ks[1], (D_MODEL,), 0.02)
    l0_w_in = nrm(ks[2], (D_MODEL, EVEN_IN), D_MODEL ** -0.5)
    l0_s5_lambda_re = -0.5 + nrm(ks[3], (S5_GROUPS, S5_STATE), 0.01)
    l0_s5_lambda_im = (jnp.pi * jnp.broadcast_to(jnp.arange(S5_STATE, dtype=F32), (S5_GROUPS, S5_STATE))
                       + nrm(ks[4], (S5_GROUPS, S5_STATE), 0.01))
    l0_s5_log_step = jax.random.uniform(ks[5], (S5_GROUPS,), F32, math.log(1e-3), math.log(1e-1))
    l0_s5_b_re = nrm(ks[6], (S5_GROUPS, S5_STATE, S5_GROUP), (2 * S5_GROUP) ** -0.5)
    l0_s5_b_im = nrm(ks[7], (S5_GROUPS, S5_STATE, S5_GROUP), (2 * S5_GROUP) ** -0.5)
    l0_s5_c_re = nrm(ks[8], (S5_GROUPS, S5_GROUP, S5_STATE), S5_STATE ** -0.5)
    l0_s5_c_im = nrm(ks[9], (S5_GROUPS, S5_GROUP, S5_STATE), S5_STATE ** -0.5)
    l0_s5_d = nrm(ks[10], (S5_GROUPS, S5_GROUP), 1.0)
    l0_s5_w_glu = nrm(ks[11], (S5_WIDTH, S5_WIDTH), S5_WIDTH ** -0.5)
    l0_s5_b_glu = nrm(ks[12], (S5_WIDTH,), 0.01)
    l0_ssd_conv_w = nrm(ks[13], (SSD_CONV, SSD_XBC), SSD_CONV ** -0.5)
    l0_ssd_conv_b = nrm(ks[14], (SSD_XBC,), 0.01)
    dt0 = jnp.exp(jax.random.uniform(ks[15], (SSD_HEADS,), F32, math.log(1e-3), math.log(1e-1)))
    l0_ssd_dt_bias = dt0 + jnp.log(-jnp.expm1(-dt0))
    l0_ssd_a_log = jnp.log(jax.random.uniform(ks[16], (SSD_HEADS,), F32, 1.0, 16.0))
    l0_ssd_d = 1.0 + nrm(ks[17], (SSD_HEADS,), 0.01)
    l0_ssd_norm_w = 1.0 + nrm(ks[18], (SSD_WIDTH,), 0.02)
    l0_w_out = nrm(ks[19], (MIX_WIDTH, D_MODEL), MIX_WIDTH ** -0.5)
    l1_norm_w = 1.0 + nrm(ks[20], (D_MODEL,), 0.02)
    l1_w_in = nrm(ks[21], (D_MODEL, ODD_IN), D_MODEL ** -0.5)
    l1_fox_b_f = jnp.log(jnp.exp(jax.random.uniform(ks[22], (FOX_HEADS,), F32, math.log(8.0), math.log(2048.0))))
    l1_w_out = nrm(ks[23], (FOX_WIDTH, D_MODEL), FOX_WIDTH ** -0.5)
    final_norm_w = 1.0 + nrm(ks[24], (D_MODEL,), 0.02)
    return {
        "x": x,
        "l0_norm_w": l0_norm_w, "l0_w_in": l0_w_in,
        "l0_s5_lambda_re": l0_s5_lambda_re, "l0_s5_lambda_im": l0_s5_lambda_im,
        "l0_s5_log_step": l0_s5_log_step,
        "l0_s5_b_re": l0_s5_b_re, "l0_s5_b_im": l0_s5_b_im,
        "l0_s5_c_re": l0_s5_c_re, "l0_s5_c_im": l0_s5_c_im,
        "l0_s5_d": l0_s5_d, "l0_s5_w_glu": l0_s5_w_glu, "l0_s5_b_glu": l0_s5_b_glu,
        "l0_ssd_conv_w": l0_ssd_conv_w, "l0_ssd_conv_b": l0_ssd_conv_b,
        "l0_ssd_dt_bias": l0_ssd_dt_bias, "l0_ssd_a_log": l0_ssd_a_log,
        "l0_ssd_d": l0_ssd_d, "l0_ssd_norm_w": l0_ssd_norm_w,
        "l0_w_out": l0_w_out,
        "l1_norm_w": l1_norm_w, "l1_w_in": l1_w_in, "l1_fox_b_f": l1_fox_b_f,
        "l1_w_out": l1_w_out,
        "final_norm_w": final_norm_w,
    }


def reference(x, l0_norm_w, l0_w_in, l0_s5_lambda_re, l0_s5_lambda_im, l0_s5_log_step,
              l0_s5_b_re, l0_s5_b_im, l0_s5_c_re, l0_s5_c_im, l0_s5_d, l0_s5_w_glu,
              l0_s5_b_glu, l0_ssd_conv_w, l0_ssd_conv_b, l0_ssd_dt_bias, l0_ssd_a_log,
              l0_ssd_d, l0_ssd_norm_w, l0_w_out, l1_norm_w, l1_w_in, l1_fox_b_f, l1_w_out,
              final_norm_w):
    layers = [
        (l0_norm_w, l0_w_in, l0_s5_lambda_re, l0_s5_lambda_im, l0_s5_log_step,
         l0_s5_b_re, l0_s5_b_im, l0_s5_c_re, l0_s5_c_im, l0_s5_d, l0_s5_w_glu,
         l0_s5_b_glu, l0_ssd_conv_w, l0_ssd_conv_b, l0_ssd_dt_bias, l0_ssd_a_log,
         l0_ssd_d, l0_ssd_norm_w, l0_w_out),
        (l1_norm_w, l1_w_in, l1_fox_b_f, l1_w_out),
    ]
    for layer in range(DEPTH):
        if layer % 2 == 0:
            x = x + ssm_layer(x, *layers[layer])
        else:
            x = x + fox_layer(x, *layers[layer])
    return rms_norm(x, final_norm_w)
```

```python
import functools
import math

import numpy as np
import jax
import jax.numpy as jnp
from jax import lax
from jax.experimental import pallas as pl
from jax.experimental.pallas import tpu as pltpu

F32 = jnp.float32
BF16 = jnp.bfloat16

NORM_EPS = 1e-5
S5_EIG_CLIP = -1e-4
SSD_GROUPS = 8
LANES = 128
SUBLANES = 8
VMEM_LIMIT = 56 * 1024 * 1024
LOG2E = 1.4426950408889634


def _cparams(sem):
    return pltpu.CompilerParams(dimension_semantics=sem, vmem_limit_bytes=VMEM_LIMIT)


def _pick(n, pref):
    if n <= pref:
        return n
    t = pref
    while t >= LANES:
        if n % t == 0:
            return t
        t -= LANES
    return n


def _sigmoid(x):
    return 1.0 / (1.0 + jnp.exp(-x))


def _silu(x):
    return x * _sigmoid(x)


def _softplus(x):
    return jnp.maximum(x, 0.0) + jnp.log1p(jnp.exp(-jnp.abs(x)))


def _rmsnorm_kernel(x_ref, w_ref, o_ref):
    x = x_ref[...]
    ms = jnp.mean(x * x, axis=-1, keepdims=True)
    o_ref[...] = ((x * lax.rsqrt(ms + NORM_EPS)) * w_ref[...]).astype(o_ref.dtype)


def rmsnorm(x2d, w, out_dtype):
    t, d = x2d.shape
    tm = _pick(t, 256)
    return pl.pallas_call(
        _rmsnorm_kernel,
        out_shape=jax.ShapeDtypeStruct((t, d), out_dtype),
        grid=(t // tm,),
        in_specs=[pl.BlockSpec((tm, d), lambda i: (i, 0)),
                  pl.BlockSpec((1, d), lambda i: (0, 0))],
        out_specs=pl.BlockSpec((tm, d), lambda i: (i, 0)),
        compiler_params=_cparams(("parallel",)),
        name="rmsnorm",
    )(x2d, w.reshape(1, d).astype(F32))


def _mm_kernel(a_ref, b_ref, *rest, has_res):
    if has_res:
        r_ref, o_ref, acc_ref = rest
    else:
        o_ref, acc_ref = rest
    k = pl.program_id(2)

    @pl.when(k == 0)
    def _():
        acc_ref[...] = jnp.zeros_like(acc_ref)

    acc_ref[...] += jnp.dot(a_ref[...], b_ref[...], preferred_element_type=F32)

    @pl.when(k == pl.num_programs(2) - 1)
    def _():
        acc = acc_ref[...]
        if has_res:
            acc = r_ref[...] + acc
        o_ref[...] = acc.astype(o_ref.dtype)


def matmul(a, b, out_dtype, res=None, tm=1024, tn=1024, tk=512):
    m, kd = a.shape
    _, n = b.shape
    tm, tn, tk = _pick(m, tm), _pick(n, tn), _pick(kd, tk)
    in_specs = [pl.BlockSpec((tm, tk), lambda i, j, k: (i, k)),
                pl.BlockSpec((tk, tn), lambda i, j, k: (k, j))]
    args = [a, b]
    if res is not None:
        in_specs.append(pl.BlockSpec((tm, tn), lambda i, j, k: (i, j)))
        args.append(res)
    return pl.pallas_call(
        functools.partial(_mm_kernel, has_res=res is not None),
        out_shape=jax.ShapeDtypeStruct((m, n), out_dtype),
        grid=(m // tm, n // tn, kd // tk),
        in_specs=in_specs,
        out_specs=pl.BlockSpec((tm, tn), lambda i, j, k: (i, j)),
        scratch_shapes=[pltpu.VMEM((tm, tn), F32)],
        compiler_params=_cparams(("parallel", "parallel", "arbitrary")),
        name="matmul",
    )(*args)


def _s5_prep_kernel(lr_ref, li_ref, st_ref, br_ref, bi_ref, lrf_ref, lif_ref, stf_ref,
                    bbr_ref, bbi_ref, pre_ref, pim_ref):
    lr = jnp.minimum(lr_ref[...], S5_EIG_CLIP)
    li = li_ref[...]
    step = jnp.exp(st_ref[...])
    mag = jnp.exp(lr * step)
    ab_re = mag * jnp.cos(li * step)
    ab_im = mag * jnp.sin(li * step)
    denom = lr * lr + li * li
    nr = ab_re - 1.0
    ni = ab_im
    coef_re = (nr * lr + ni * li) / denom
    coef_im = (ni * lr - nr * li) / denom
    br = br_ref[...]
    bi = bi_ref[...]
    bbr_ref[...] = coef_re * br - coef_im * bi
    bbi_ref[...] = coef_re * bi + coef_im * br
    lrf = jnp.minimum(lrf_ref[...], S5_EIG_CLIP)
    stepf = jnp.exp(stf_ref[...])
    n = pre_ref.shape[0]
    kk = (lax.broadcasted_iota(jnp.int32, (n, 1), 0) + 1).astype(F32)
    magp = jnp.exp(kk * (lrf * stepf))
    ang = kk * (lif_ref[...] * stepf)
    pre_ref[...] = magp * jnp.cos(ang)
    pim_ref[...] = magp * jnp.sin(ang)


def _s5_scan_kernel(u_ref, pm_ref, pmt_ref, bre_ref, bim_ref, cre_ref, cim_ref, d_ref,
                    pre_ref, pim_ref, o_ref, car_re, car_im, s_re, s_im, sb_re, sb_im, *, n):
    c = pl.program_id(2)

    @pl.when(c == 0)
    def _():
        car_re[...] = jnp.zeros_like(car_re)
        car_im[...] = jnp.zeros_like(car_im)

    sl = lambda i: pl.ds(i * SUBLANES, SUBLANES)
    up = jnp.dot(pm_ref[...], u_ref[...], preferred_element_type=F32)
    upb = up.astype(BF16)
    s_re[...] = jnp.dot(upb, bre_ref[...], preferred_element_type=F32)
    s_im[...] = jnp.dot(upb, bim_ref[...], preferred_element_type=F32)
    w = s_re.shape[1]
    a_re = jnp.broadcast_to(pre_ref[0:1, :], (SUBLANES, w))
    a_im = jnp.broadcast_to(pim_ref[0:1, :], (SUBLANES, w))
    x_re = jnp.zeros((SUBLANES, w), F32)
    x_im = jnp.zeros((SUBLANES, w), F32)
    for i in range(n):
        n_re = a_re * x_re - a_im * x_im + s_re[sl(i), :]
        n_im = a_re * x_im + a_im * x_re + s_im[sl(i), :]
        x_re, x_im = n_re, n_im
        s_re[sl(i), :] = x_re
        s_im[sl(i), :] = x_im
    an_re = pre_ref[n - 1:n, :]
    an_im = pim_ref[n - 1:n, :]
    c_re = car_re[...]
    c_im = car_im[...]
    rows_re, rows_im = [c_re], [c_im]
    for s in range(1, SUBLANES + 1):
        e_re = x_re[s - 1:s, :]
        e_im = x_im[s - 1:s, :]
        c_re, c_im = (e_re + an_re * c_re - an_im * c_im,
                      e_im + an_re * c_im + an_im * c_re)
        if s < SUBLANES:
            rows_re.append(c_re)
            rows_im.append(c_im)
    car_re[...] = c_re
    car_im[...] = c_im
    cm_re = jnp.concatenate(rows_re, axis=0)
    cm_im = jnp.concatenate(rows_im, axis=0)
    for i in range(n):
        p_re = jnp.broadcast_to(pre_ref[i:i + 1, :], (SUBLANES, w))
        p_im = jnp.broadcast_to(pim_ref[i:i + 1, :], (SUBLANES, w))
        sb_re[sl(i), :] = (s_re[sl(i), :] + p_re * cm_re - p_im * cm_im).astype(BF16)
        sb_im[sl(i), :] = (s_im[sl(i), :] + p_re * cm_im + p_im * cm_re).astype(BF16)
    yp = (jnp.dot(sb_re[...], cre_ref[...], preferred_element_type=F32)
          - jnp.dot(sb_im[...], cim_ref[...], preferred_element_type=F32)
          + d_ref[...] * up)
    y = jnp.dot(pmt_ref[...], yp.astype(BF16), preferred_element_type=F32)
    cdf = 0.5 * (1.0 + jnp.tanh(np.float32(math.sqrt(2.0 / math.pi)) * (y + 0.044715 * (y * y * y))))
    o_ref[...] = (y * cdf).astype(o_ref.dtype)


def _s5_perm(tc):
    n = tc // SUBLANES
    pm = np.zeros((tc, tc), np.float32)
    for i in range(n):
        for s in range(SUBLANES):
            pm[i * SUBLANES + s, s * n + i] = 1.0
    return pm


def s5_scan(ug, lam_re, lam_im, log_step, b_re, b_im, c_re, c_im, d, bsz, seqlen, tc=256):
    g, p = lam_re.shape
    hh = b_re.shape[2]
    width = g * hh
    gpt = LANES // hh
    nt = width // LANES
    sw = gpt * p
    tc = min(tc, seqlen)
    n = tc // SUBLANES
    nc = seqlen // tc
    t = bsz * seqlen

    rep = lambda a: jnp.repeat(a.astype(F32), hh, axis=1)
    lr_rep, li_rep = rep(lam_re), rep(lam_im)
    st_rep = jnp.broadcast_to(log_step.astype(F32)[:, None], (g, p * hh))
    flat = lambda a: a.astype(F32).reshape(1, g * p)
    st_flat = jnp.broadcast_to(log_step.astype(F32)[:, None], (g, p)).reshape(1, g * p)
    bb_re, bb_im, ptab_re, ptab_im = pl.pallas_call(
        _s5_prep_kernel,
        out_shape=[jax.ShapeDtypeStruct((g, p * hh), F32)] * 2
        + [jax.ShapeDtypeStruct((n, g * p), F32)] * 2,
        name="s5_prep",
    )(lr_rep, li_rep, st_rep, b_re.astype(F32).reshape(g, p * hh), b_im.astype(F32).reshape(g, p * hh),
      flat(lam_re), flat(lam_im), st_flat)

    eye = jnp.eye(gpt, dtype=F32)
    bd_b = lambda bb: jnp.einsum('jgph,gk->jghkp', bb.reshape(nt, gpt, p, hh), eye
                                 ).reshape(nt, LANES, sw).astype(BF16)
    bd_c = lambda cc: jnp.einsum('jghp,gk->jkpgh', cc.astype(F32).reshape(nt, gpt, hh, p), eye
                                 ).reshape(nt, sw, LANES).astype(BF16)
    pm = _s5_perm(tc)
    tile3 = lambda r, c: pl.BlockSpec((None, r, c), lambda b, j, k: (j, 0, 0))
    full = lambda r, c: pl.BlockSpec((r, c), lambda b, j, k: (0, 0))
    return pl.pallas_call(
        functools.partial(_s5_scan_kernel, n=n),
        out_shape=jax.ShapeDtypeStruct((t, width), BF16),
        grid=(bsz, nt, nc),
        in_specs=[pl.BlockSpec((tc, LANES), lambda b, j, k: (b * nc + k, j)),
                  full(tc, tc), full(tc, tc),
                  tile3(LANES, sw), tile3(LANES, sw), tile3(sw, LANES), tile3(sw, LANES),
                  tile3(1, LANES),
                  pl.BlockSpec((n, sw), lambda b, j, k: (0, j)),
                  pl.BlockSpec((n, sw), lambda b, j, k: (0, j))],
        out_specs=pl.BlockSpec((tc, LANES), lambda b, j, k: (b * nc + k, j)),
        scratch_shapes=[pltpu.VMEM((1, sw), F32), pltpu.VMEM((1, sw), F32),
                        pltpu.VMEM((tc, sw), F32), pltpu.VMEM((tc, sw), F32),
                        pltpu.VMEM((tc, sw), BF16), pltpu.VMEM((tc, sw), BF16)],
        compiler_params=_cparams(("parallel", "parallel", "arbitrary")),
        name="s5_scan",
    )(ug, jnp.asarray(pm, BF16), jnp.asarray(pm.T, BF16), bd_b(bb_re), bd_b(bb_im),
      bd_c(c_re), bd_c(c_im), d.astype(F32).reshape(nt, 1, LANES), ptab_re, ptab_im)


def _glu_kernel(a_ref, w_ref, b_ref, gn_ref, gate_ref, mixed_ref, o_ref):
    del mixed_ref
    z = jnp.dot(a_ref[...], w_ref[...], preferred_element_type=F32) + b_ref[...]
    gn = gn_ref[...].astype(F32)
    o_ref[...] = ((gn * _sigmoid(z)) * _silu(gate_ref[...].astype(F32))).astype(o_ref.dtype)


def s5_glu(gact, w_glu, b_glu, ug, mixed, col0, tm=512, tn=512):
    t, width = gact.shape
    tm, tn = _pick(t, tm), _pick(width, tn)
    while col0 % tn:
        tn -= LANES
    nj = width // tn
    return pl.pallas_call(
        _glu_kernel,
        out_shape=jax.ShapeDtypeStruct(mixed.shape, mixed.dtype),
        grid=(t // tm, nj),
        in_specs=[pl.BlockSpec((tm, width), lambda i, j: (i, 0)),
                  pl.BlockSpec((width, tn), lambda i, j: (0, j)),
                  pl.BlockSpec((1, tn), lambda i, j: (0, j)),
                  pl.BlockSpec((tm, tn), lambda i, j: (i, j)),
                  pl.BlockSpec((tm, tn), lambda i, j: (i, nj + j)),
                  pl.BlockSpec(memory_space=pl.ANY)],
        out_specs=pl.BlockSpec((tm, tn), lambda i, j: (i, col0 // tn + j)),
        input_output_aliases={5: 0},
        compiler_params=_cparams(("parallel", "arbitrary")),
        name="s5_glu",
    )(gact, w_glu.astype(BF16), b_glu.astype(F32).reshape(1, width), gact, ug, mixed)


def _conv_kernel(cur_ref, tail_ref, w_ref, b_ref, o_ref, *, halo):
    c = pl.program_id(1)
    cur = cur_ref[...].astype(F32)
    tail = jnp.where(c > 0, tail_ref[...].astype(F32), 0.0)
    ext = jnp.concatenate([tail, cur], axis=0)
    kw = w_ref.shape[0]
    acc = b_ref[...] + w_ref[kw - 1:kw, :] * cur
    for k in range(kw - 1):
        acc = acc + w_ref[k:k + 1, :] * pltpu.roll(ext, kw - 1 - k, axis=0)[halo:, :]
    o_ref[...] = _silu(acc).astype(o_ref.dtype)


def ssd_conv(xbc, conv_w, conv_b, bsz, seqlen, tc=512, wc=1024):
    t, ch = xbc.shape
    tc, wc = _pick(seqlen, tc), _pick(ch, wc)
    nc = seqlen // tc
    halo = 16
    hb = tc // halo
    return pl.pallas_call(
        functools.partial(_conv_kernel, halo=halo),
        out_shape=jax.ShapeDtypeStruct((t, ch), BF16),
        grid=(bsz, nc, ch // wc),
        in_specs=[pl.BlockSpec((tc, wc), lambda b, c, w: (b * nc + c, w)),
                  pl.BlockSpec((halo, wc), lambda b, c, w: (jnp.maximum((b * nc + c) * hb - 1, 0), w)),
                  pl.BlockSpec((conv_w.shape[0], wc), lambda b, c, w: (0, w)),
                  pl.BlockSpec((1, wc), lambda b, c, w: (0, w))],
        out_specs=pl.BlockSpec((tc, wc), lambda b, c, w: (b * nc + c, w)),
        compiler_params=_cparams(("parallel", "parallel", "parallel")),
        name="ssd_conv",
    )(xbc, xbc, conv_w.astype(F32), conv_b.astype(F32).reshape(1, ch))


def _expand(v, e):
    hi = v.astype(BF16)
    lo = (v - hi.astype(F32)).astype(BF16)
    return (jnp.dot(hi, e, preferred_element_type=F32)
            + jnp.dot(lo, e, preferred_element_type=F32))


def _ssd_kernel(x_ref, b_ref, c_ref, z_ref, dt_ref, dtb_ref, alog_ref, e_ref, sel_ref,
                d_ref, nw_ref, tri_ref, o_ref, st_ref, *, hpg, hd):
    ck = pl.program_id(2)

    @pl.when(ck == 0)
    def _():
        st_ref[...] = jnp.zeros_like(st_ref)

    q = x_ref.shape[0]
    e = e_ref[...]
    dt = _softplus(dt_ref[...] + dtb_ref[...])
    la = dt * (-jnp.exp(alog_ref[...]))
    cum = jnp.dot(tri_ref[...], la, preferred_element_type=F32, precision=lax.Precision.HIGHEST)
    cum_last = cum[q - 1:q, :]
    dec_end = jnp.exp(cum_last - cum)
    dec_out = jnp.exp(cum)
    chunk_dec = jnp.exp(jnp.broadcast_to(cum_last, (SUBLANES, LANES)))

    x = x_ref[...].astype(F32)
    xdt = x * _expand(dt, e)
    xdt_b = xdt.astype(BF16)
    xde = (xdt * _expand(dec_end, e)).astype(BF16)
    bm = b_ref[...]
    cm = c_ref[...]
    scores = lax.dot_general(cm, bm, (((1,), (1,)), ((), ())), preferred_element_type=F32)

    cum_g = jnp.dot(cum, sel_ref[...], preferred_element_type=F32, precision=lax.Precision.HIGHEST)
    cum_t = cum_g.T
    causal = (lax.broadcasted_iota(jnp.int32, (q, q), 0) >= lax.broadcasted_iota(jnp.int32, (q, q), 1))
    lane = lax.broadcasted_iota(jnp.int32, (q, LANES), 1)
    hpt = LANES // hd
    parts = []
    for tl in range(hpg // hpt):
        xt = xdt_b[:, tl * LANES:(tl + 1) * LANES]
        acc = None
        for hh in range(hpt):
            r = tl * hpt + hh
            seg = cum_g[:, r:r + 1] - cum_t[r:r + 1, :]
            wgt = (jnp.where(causal, jnp.exp(seg), 0.0) * scores).astype(BF16)
            xm = jnp.where((lane >= hh * hd) & (lane < (hh + 1) * hd), xt, jnp.zeros_like(xt))
            part = jnp.dot(wgt, xm, preferred_element_type=F32)
            acc = part if acc is None else acc + part
        parts.append(acc)
    y_diag = jnp.concatenate(parts, axis=1)

    st = st_ref[...]
    y_off = jnp.dot(cm, st.astype(BF16), preferred_element_type=F32) * _expand(dec_out, e)
    upd = lax.dot_general(bm, xde, (((0,), (0,)), ((), ())), preferred_element_type=F32)
    st_ref[...] = st * _expand(chunk_dec, e)[0:1, :] + upd

    y = y_diag + y_off + d_ref[...] * x
    y = y * _silu(z_ref[...].astype(F32))
    ms = jnp.mean(y * y, axis=-1, keepdims=True)
    o_ref[...] = ((y * lax.rsqrt(ms + NORM_EPS)) * nw_ref[...]).astype(o_ref.dtype)


def ssd_mix(xa, z, dt_raw, dt_bias, a_log, d, norm_w, mix_width, bsz, seqlen, q=128):
    t, w = z.shape
    nh = dt_bias.shape[0]
    hd = w // nh
    ng = SSD_GROUPS
    hpg = nh // ng
    gw = hpg * hd
    ns = (xa.shape[1] - w) // (2 * ng)
    assert ns == LANES and nh <= LANES and LANES % hd == 0 and hpg % (LANES // hd) == 0
    q = min(q, seqlen)
    nc = seqlen // q
    pad = lambda a: jnp.zeros((1, LANES), F32).at[0, :nh].set(a.astype(F32))
    e_np = np.zeros((ng, LANES, gw), np.float32)
    sel_np = np.zeros((ng, LANES, LANES), np.float32)
    for g in range(ng):
        for r in range(hpg):
            e_np[g, g * hpg + r, r * hd:(r + 1) * hd] = 1.0
            sel_np[g, g * hpg + r, r] = 1.0
    tri = np.tril(np.ones((q, q), np.float32))
    row = lambda b, g, c: b * nc + c
    return pl.pallas_call(
        functools.partial(_ssd_kernel, hpg=hpg, hd=hd),
        out_shape=jax.ShapeDtypeStruct((t, mix_width), BF16),
        grid=(bsz, ng, nc),
        in_specs=[pl.BlockSpec((q, gw), lambda b, g, c: (row(b, g, c), g)),
                  pl.BlockSpec((q, ns), lambda b, g, c: (row(b, g, c), w // ns + g)),
                  pl.BlockSpec((q, ns), lambda b, g, c: (row(b, g, c), w // ns + ng + g)),
                  pl.BlockSpec((q, gw), lambda b, g, c: (row(b, g, c), g)),
                  pl.BlockSpec((q, LANES), lambda b, g, c: (row(b, g, c), 0)),
                  pl.BlockSpec((1, LANES), lambda b, g, c: (0, 0)),
                  pl.BlockSpec((1, LANES), lambda b, g, c: (0, 0)),
                  pl.BlockSpec((None, LANES, gw), lambda b, g, c: (g, 0, 0)),
                  pl.BlockSpec((None, LANES, LANES), lambda b, g, c: (g, 0, 0)),
                  pl.BlockSpec((1, gw), lambda b, g, c: (0, g)),
                  pl.BlockSpec((1, gw), lambda b, g, c: (0, g)),
                  pl.BlockSpec((q, q), lambda b, g, c: (0, 0))],
        out_specs=pl.BlockSpec((q, gw), lambda b, g, c: (row(b, g, c), g)),
        scratch_shapes=[pltpu.VMEM((ns, gw), F32)],
        compiler_params=_cparams(("parallel", "parallel", "arbitrary")),
        name="ssd_mix",
    )(xa, xa, xa, z, dt_raw, pad(dt_bias), pad(a_log), jnp.asarray(e_np, BF16), jnp.asarray(sel_np),
      jnp.repeat(d.astype(F32), hd).reshape(1, w), norm_w.astype(F32).reshape(1, w), jnp.asarray(tri))


def _logf_cumsum_kernel(f_ref, b_ref, tri_ref, o_ref, car_ref):
    c = pl.program_id(1)

    @pl.when(c == 0)
    def _():
        car_ref[...] = jnp.zeros_like(car_ref)

    lf = -_softplus(-(f_ref[...] + b_ref[...]))
    cum = jnp.dot(tri_ref[...], lf, preferred_element_type=F32,
                  precision=lax.Precision.HIGHEST) + car_ref[...]
    o_ref[...] = cum
    car_ref[...] = cum[cum.shape[0] - 1:, :]


def logf_cumsum(f_raw, b_f, bsz, seqlen, tc=256):
    t = f_raw.shape[0]
    nh = b_f.shape[0]
    tc = min(tc, seqlen)
    nc = seqlen // tc
    b_pad = jnp.zeros((1, LANES), F32).at[0, :nh].set(b_f.astype(F32))
    tri = np.tril(np.ones((tc, tc), np.float32))
    return pl.pallas_call(
        _logf_cumsum_kernel,
        out_shape=jax.ShapeDtypeStruct((t, LANES), F32),
        grid=(bsz, nc),
        in_specs=[pl.BlockSpec((tc, LANES), lambda b, c: (b * nc + c, 0)),
                  pl.BlockSpec((1, LANES), lambda b, c: (0, 0)),
                  pl.BlockSpec((tc, tc), lambda b, c: (0, 0))],
        out_specs=pl.BlockSpec((tc, LANES), lambda b, c: (b * nc + c, 0)),
        scratch_shapes=[pltpu.VMEM((1, LANES), F32)],
        compiler_params=_cparams(("parallel", "arbitrary")),
        name="logf_cumsum",
    )(f_raw, b_pad, jnp.asarray(tri))


def _fox_kernel(q_ref, k_ref, v_ref, ck_ref, gate_ref, o_ref, qs_ref, m_ref, l_ref, acc_ref, *, scale):
    qi = pl.program_id(2)
    kj = pl.program_id(3)
    tq, tk = q_ref.shape[0], k_ref.shape[0]

    @pl.when(kj == 0)
    def _():
        qs_ref[...] = (q_ref[...].astype(F32) * (scale * LOG2E)).astype(BF16)
        m_ref[...] = jnp.full_like(m_ref, -jnp.inf)
        l_ref[...] = jnp.zeros_like(l_ref)
        acc_ref[...] = jnp.zeros_like(acc_ref)

    def step(masked):
        s = lax.dot_general(qs_ref[...], k_ref[...], (((1,), (1,)), ((), ())), preferred_element_type=F32)
        t = s - ck_ref[...] * LOG2E
        if masked:
            keep = (lax.broadcasted_iota(jnp.int32, (tq, tk), 0) >= lax.broadcasted_iota(jnp.int32, (tq, tk), 1))
            t = jnp.where(keep, t, -jnp.inf)
        m_prev = m_ref[...]
        m_new = jnp.maximum(m_prev, jnp.max(t, axis=1, keepdims=True))
        alpha = jnp.exp2(m_prev - m_new)
        p = jnp.exp2(t - m_new)
        l_ref[...] = alpha * l_ref[...] + jnp.sum(p, axis=1, keepdims=True)
        acc_ref[...] = alpha * acc_ref[...] + jnp.dot(p.astype(BF16), v_ref[...], preferred_element_type=F32)
        m_ref[...] = m_new

    @pl.when(kj < qi)
    def _():
        step(False)

    @pl.when(kj == qi)
    def _():
        step(True)

    @pl.when(kj == pl.num_programs(3) - 1)
    def _():
        att = acc_ref[...] / l_ref[...]
        o_ref[...] = (att * _silu(gate_ref[...].astype(F32))).astype(o_ref.dtype)


def fox_attention(qkvg, ck, nh, hd, bsz, seqlen, tb=512):
    t = qkvg.shape[0]
    tb = min(tb, seqlen)
    nb = seqlen // tb
    kmap = lambda off: (lambda b, h, i, j: (b * nb + jnp.minimum(j, i), off + h))
    return pl.pallas_call(
        functools.partial(_fox_kernel, scale=1.0 / math.sqrt(hd)),
        out_shape=jax.ShapeDtypeStruct((t, nh * hd), BF16),
        grid=(bsz, nh, nb, nb),
        in_specs=[pl.BlockSpec((tb, hd), lambda b, h, i, j: (b * nb + i, h)),
                  pl.BlockSpec((tb, hd), kmap(nh)),
                  pl.BlockSpec((tb, hd), kmap(2 * nh)),
                  pl.BlockSpec((None, None, 1, tb), lambda b, h, i, j: (b, h, 0, jnp.minimum(j, i))),
                  pl.BlockSpec((tb, hd), lambda b, h, i, j: (b * nb + i, 3 * nh + h))],
        out_specs=pl.BlockSpec((tb, hd), lambda b, h, i, j: (b * nb + i, h)),
        scratch_shapes=[pltpu.VMEM((tb, hd), BF16), pltpu.VMEM((tb, 1), F32),
                        pltpu.VMEM((tb, 1), F32), pltpu.VMEM((tb, hd), F32)],
        compiler_params=_cparams(("parallel", "parallel", "parallel", "arbitrary")),
        name="fox_attention",
    )(qkvg, qkvg, qkvg, ck, qkvg)


def _pad_cols(w, n):
    return jnp.pad(w, ((0, 0), (0, n - w.shape[1])))


def kernel(x, l0_norm_w, l0_w_in, l0_s5_lambda_re, l0_s5_lambda_im, l0_s5_log_step, l0_s5_b_re, l0_s5_b_im, l0_s5_c_re, l0_s5_c_im, l0_s5_d, l0_s5_w_glu, l0_s5_b_glu, l0_ssd_conv_w, l0_ssd_conv_b, l0_ssd_dt_bias, l0_ssd_a_log, l0_ssd_d, l0_ssd_norm_w, l0_w_out, l1_norm_w, l1_w_in, l1_fox_b_f, l1_w_out, final_norm_w):
    bsz, seqlen, dm = x.shape
    t = bsz * seqlen
    x2 = x.reshape(t, dm)

    s5_w = l0_s5_w_glu.shape[0]
    ssd_w = l0_ssd_norm_w.shape[0]
    xbc_w = l0_ssd_conv_w.shape[1]
    ssd_h = l0_ssd_dt_bias.shape[0]
    mix_w = s5_w + ssd_w
    o_z = 2 * s5_w
    o_xbc = o_z + ssd_w
    o_dt = o_xbc + xbc_w

    h0 = rmsnorm(x2, l0_norm_w, BF16)
    w0 = l0_w_in.astype(BF16)
    ug = matmul(h0, w0[:, :o_z], BF16)
    z = matmul(h0, w0[:, o_z:o_xbc], BF16)
    xbc = matmul(h0, w0[:, o_xbc:o_dt], BF16)
    dt_raw = matmul(h0, _pad_cols(w0[:, o_dt:o_dt + ssd_h], LANES), F32)

    gact = s5_scan(ug, l0_s5_lambda_re, l0_s5_lambda_im, l0_s5_log_step, l0_s5_b_re, l0_s5_b_im,
                   l0_s5_c_re, l0_s5_c_im, l0_s5_d, bsz, seqlen)
    xa = ssd_conv(xbc, l0_ssd_conv_w, l0_ssd_conv_b, bsz, seqlen)
    mixed = ssd_mix(xa, z, dt_raw, l0_ssd_dt_bias, l0_ssd_a_log, l0_ssd_d, l0_ssd_norm_w,
                    mix_w, bsz, seqlen)
    mixed = s5_glu(gact, l0_s5_w_glu, l0_s5_b_glu, ug, mixed, ssd_w)
    w_out0 = jnp.concatenate([l0_w_out[s5_w:], l0_w_out[:s5_w]], axis=0).astype(BF16)
    x1 = matmul(mixed, w_out0, F32, res=x2)

    fox_w = l1_w_out.shape[0]
    nh = l1_fox_b_f.shape[0]
    hd = fox_w // nh
    h1 = rmsnorm(x1, l1_norm_w, BF16)
    w1 = l1_w_in.astype(BF16)
    qkvg = matmul(h1, w1[:, :4 * fox_w], BF16)
    f_raw = matmul(h1, _pad_cols(w1[:, 4 * fox_w:4 * fox_w + nh], LANES), F32)
    cum = logf_cumsum(f_raw, l1_fox_b_f, bsz, seqlen)
    ck = cum[:, :nh].reshape(bsz, seqlen, nh).transpose(0, 2, 1).reshape(bsz, nh, 1, seqlen)
    att = fox_attention(qkvg, ck, nh, hd, bsz, seqlen)
    x2_out = matmul(att, l1_w_out.astype(BF16), F32, res=x1)

    return rmsnorm(x2_out, final_norm_w, F32).reshape(bsz, seqlen, dm)
```

```python
import functools
import math

import numpy as np
import jax
import jax.numpy as jnp
from jax import lax
from jax.experimental import pallas as pl
from jax.experimental.pallas import tpu as pltpu

F32 = jnp.float32
BF16 = jnp.bfloat16

NORM_EPS = 1e-5
S5_EIG_CLIP = -1e-4
SSD_GROUPS = 8
LANES = 128
SUBLANES = 8
VMEM_LIMIT = 56 * 1024 * 1024
LOG2E = 1.4426950408889634


def _cparams(sem):
    return pltpu.CompilerParams(dimension_semantics=sem, vmem_limit_bytes=VMEM_LIMIT)


def _pick(n, pref):
    if n <= pref:
        return n
    t = pref
    while t >= LANES:
        if n % t == 0:
            return t
        t -= LANES
    return n


def _sigmoid(x):
    return 1.0 / (1.0 + jnp.exp(-x))


def _silu(x):
    return x * _sigmoid(x)


def _softplus(x):
    return jnp.maximum(x, 0.0) + jnp.log1p(jnp.exp(-jnp.abs(x)))


def _rmsnorm_kernel(x_ref, w_ref, o_ref):
    x = x_ref[...]
    ms = jnp.mean(x * x, axis=-1, keepdims=True)
    o_ref[...] = ((x * lax.rsqrt(ms + NORM_EPS)) * w_ref[...]).astype(o_ref.dtype)


def rmsnorm(x2d, w, out_dtype):
    t, d = x2d.shape
    tm = _pick(t, 256)
    return pl.pallas_call(
        _rmsnorm_kernel,
        out_shape=jax.ShapeDtypeStruct((t, d), out_dtype),
        grid=(t // tm,),
        in_specs=[pl.BlockSpec((tm, d), lambda i: (i, 0)),
                  pl.BlockSpec((1, d), lambda i: (0, 0))],
        out_specs=pl.BlockSpec((tm, d), lambda i: (i, 0)),
        compiler_params=_cparams(("parallel",)),
        name="rmsnorm",
    )(x2d, w.reshape(1, d).astype(F32))


def _mm_kernel(a_ref, b_ref, *rest, has_res):
    if has_res:
        r_ref, o_ref = rest
    else:
        (o_ref,) = rest
    acc = jnp.dot(a_ref[...], b_ref[...], preferred_element_type=F32)
    if has_res:
        acc = r_ref[...] + acc
    o_ref[...] = acc.astype(o_ref.dtype)


def matmul(a, b, out_dtype, res=None, tm=1024, tn=1024):
    m, kd = a.shape
    _, n = b.shape
    tm, tn = _pick(m, tm), _pick(n, tn)
    in_specs = [pl.BlockSpec((tm, kd), lambda i, j: (i, 0)),
                pl.BlockSpec((kd, tn), lambda i, j: (0, j))]
    args = [a, b]
    if res is not None:
        in_specs.append(pl.BlockSpec((tm, tn), lambda i, j: (i, j)))
        args.append(res)
    return pl.pallas_call(
        functools.partial(_mm_kernel, has_res=res is not None),
        out_shape=jax.ShapeDtypeStruct((m, n), out_dtype),
        grid=(m // tm, n // tn),
        in_specs=in_specs,
        out_specs=pl.BlockSpec((tm, tn), lambda i, j: (i, j)),
        compiler_params=_cparams(("parallel", "parallel")),
        name="matmul",
    )(*args)


def _s5_prep_kernel(lr_ref, li_ref, st_ref, br_ref, bi_ref, lrf_ref, lif_ref, stf_ref,
                    bbr_ref, bbi_ref, pre_ref, pim_ref):
    lr = jnp.minimum(lr_ref[...], S5_EIG_CLIP)
    li = li_ref[...]
    step = jnp.exp(st_ref[...])
    mag = jnp.exp(lr * step)
    ab_re = mag * jnp.cos(li * step)
    ab_im = mag * jnp.sin(li * step)
    denom = lr * lr + li * li
    nr = ab_re - 1.0
    ni = ab_im
    coef_re = (nr * lr + ni * li) / denom
    coef_im = (ni * lr - nr * li) / denom
    br = br_ref[...]
    bi = bi_ref[...]
    bbr_ref[...] = coef_re * br - coef_im * bi
    bbi_ref[...] = coef_re * bi + coef_im * br
    lrf = jnp.minimum(lrf_ref[...], S5_EIG_CLIP)
    stepf = jnp.exp(stf_ref[...])
    n = pre_ref.shape[0]
    kk = (lax.broadcasted_iota(jnp.int32, (n, 1), 0) + 1).astype(F32)
    magp = jnp.exp(kk * (lrf * stepf))
    ang = kk * (lif_ref[...] * stepf)
    pre_ref[...] = magp * jnp.cos(ang)
    pim_ref[...] = magp * jnp.sin(ang)


def _s5_scan_kernel(u_ref, pm_ref, pmt_ref, bre_ref, bim_ref, cre_ref, cim_ref, d_ref,
                    pre_ref, pim_ref, o_ref, car_re, car_im, s_re, s_im, sb_re, sb_im, *, n):
    c = pl.program_id(2)

    @pl.when(c == 0)
    def _():
        car_re[...] = jnp.zeros_like(car_re)
        car_im[...] = jnp.zeros_like(car_im)

    sl = lambda i: pl.ds(i * SUBLANES, SUBLANES)
    up = jnp.dot(pm_ref[...], u_ref[...], preferred_element_type=F32)
    upb = up.astype(BF16)
    s_re[...] = jnp.dot(upb, bre_ref[...], preferred_element_type=F32)
    s_im[...] = jnp.dot(upb, bim_ref[...], preferred_element_type=F32)
    w = s_re.shape[1]
    a_re = jnp.broadcast_to(pre_ref[0:1, :], (SUBLANES, w))
    a_im = jnp.broadcast_to(pim_ref[0:1, :], (SUBLANES, w))
    x_re = jnp.zeros((SUBLANES, w), F32)
    x_im = jnp.zeros((SUBLANES, w), F32)
    for i in range(n):
        n_re = a_re * x_re - a_im * x_im + s_re[sl(i), :]
        n_im = a_re * x_im + a_im * x_re + s_im[sl(i), :]
        x_re, x_im = n_re, n_im
        s_re[sl(i), :] = x_re
        s_im[sl(i), :] = x_im
    an_re = pre_ref[n - 1:n, :]
    an_im = pim_ref[n - 1:n, :]
    c_re = car_re[...]
    c_im = car_im[...]
    rows_re, rows_im = [c_re], [c_im]
    for s in range(1, SUBLANES + 1):
        e_re = x_re[s - 1:s, :]
        e_im = x_im[s - 1:s, :]
        c_re, c_im = (e_re + an_re * c_re - an_im * c_im,
                      e_im + an_re * c_im + an_im * c_re)
        if s < SUBLANES:
            rows_re.append(c_re)
            rows_im.append(c_im)
    car_re[...] = c_re
    car_im[...] = c_im
    cm_re = jnp.concatenate(rows_re, axis=0)
    cm_im = jnp.concatenate(rows_im, axis=0)
    for i in range(n):
        p_re = jnp.broadcast_to(pre_ref[i:i + 1, :], (SUBLANES, w))
        p_im = jnp.broadcast_to(pim_ref[i:i + 1, :], (SUBLANES, w))
        sb_re[sl(i), :] = (s_re[sl(i), :] + p_re * cm_re - p_im * cm_im).astype(BF16)
        sb_im[sl(i), :] = (s_im[sl(i), :] + p_re * cm_im + p_im * cm_re).astype(BF16)
    yp = (jnp.dot(sb_re[...], cre_ref[...], preferred_element_type=F32)
          - jnp.dot(sb_im[...], cim_ref[...], preferred_element_type=F32)
          + d_ref[...] * up)
    y = jnp.dot(pmt_ref[...], yp.astype(BF16), preferred_element_type=F32)
    cdf = 0.5 * (1.0 + jnp.tanh(np.float32(math.sqrt(2.0 / math.pi)) * (y + 0.044715 * (y * y * y))))
    o_ref[...] = (y * cdf).astype(o_ref.dtype)


def _s5_perm(tc):
    n = tc // SUBLANES
    pm = np.zeros((tc, tc), np.float32)
    for i in range(n):
        for s in range(SUBLANES):
            pm[i * SUBLANES + s, s * n + i] = 1.0
    return pm


def s5_scan(ug, lam_re, lam_im, log_step, b_re, b_im, c_re, c_im, d, bsz, seqlen, tc=256):
    g, p = lam_re.shape
    hh = b_re.shape[2]
    width = g * hh
    gpt = LANES // hh
    nt = width // LANES
    sw = gpt * p
    tc = min(tc, seqlen)
    n = tc // SUBLANES
    nc = seqlen // tc
    t = bsz * seqlen

    rep = lambda a: jnp.repeat(a.astype(F32), hh, axis=1)
    lr_rep, li_rep = rep(lam_re), rep(lam_im)
    st_rep = jnp.broadcast_to(log_step.astype(F32)[:, None], (g, p * hh))
    flat = lambda a: a.astype(F32).reshape(1, g * p)
    st_flat = jnp.broadcast_to(log_step.astype(F32)[:, None], (g, p)).reshape(1, g * p)
    bb_re, bb_im, ptab_re, ptab_im = pl.pallas_call(
        _s5_prep_kernel,
        out_shape=[jax.ShapeDtypeStruct((g, p * hh), F32)] * 2
        + [jax.ShapeDtypeStruct((n, g * p), F32)] * 2,
        name="s5_prep",
    )(lr_rep, li_rep, st_rep, b_re.astype(F32).reshape(g, p * hh), b_im.astype(F32).reshape(g, p * hh),
      flat(lam_re), flat(lam_im), st_flat)

    eye = jnp.eye(gpt, dtype=F32)
    bd_b = lambda bb: jnp.einsum('jgph,gk->jghkp', bb.reshape(nt, gpt, p, hh), eye
                                 ).reshape(nt, LANES, sw).astype(BF16)
    bd_c = lambda cc: jnp.einsum('jghp,gk->jkpgh', cc.astype(F32).reshape(nt, gpt, hh, p), eye
                                 ).reshape(nt, sw, LANES).astype(BF16)
    pm = _s5_perm(tc)
    tile3 = lambda r, c: pl.BlockSpec((None, r, c), lambda b, j, k: (j, 0, 0))
    full = lambda r, c: pl.BlockSpec((r, c), lambda b, j, k: (0, 0))
    return pl.pallas_call(
        functools.partial(_s5_scan_kernel, n=n),
        out_shape=jax.ShapeDtypeStruct((t, width), BF16),
        grid=(bsz, nt, nc),
        in_specs=[pl.BlockSpec((tc, LANES), lambda b, j, k: (b * nc + k, j)),
                  full(tc, tc), full(tc, tc),
                  tile3(LANES, sw), tile3(LANES, sw), tile3(sw, LANES), tile3(sw, LANES),
                  tile3(1, LANES),
                  pl.BlockSpec((n, sw), lambda b, j, k: (0, j)),
                  pl.BlockSpec((n, sw), lambda b, j, k: (0, j))],
        out_specs=pl.BlockSpec((tc, LANES), lambda b, j, k: (b * nc + k, j)),
        scratch_shapes=[pltpu.VMEM((1, sw), F32), pltpu.VMEM((1, sw), F32),
                        pltpu.VMEM((tc, sw), F32), pltpu.VMEM((tc, sw), F32),
                        pltpu.VMEM((tc, sw), BF16), pltpu.VMEM((tc, sw), BF16)],
        compiler_params=_cparams(("parallel", "parallel", "arbitrary")),
        name="s5_scan",
    )(ug, jnp.asarray(pm, BF16), jnp.asarray(pm.T, BF16), bd_b(bb_re), bd_b(bb_im),
      bd_c(c_re), bd_c(c_im), d.astype(F32).reshape(nt, 1, LANES), ptab_re, ptab_im)


def _glu_kernel(a_ref, w_ref, b_ref, gn_ref, gate_ref, mixed_ref, o_ref):
    del mixed_ref
    z = jnp.dot(a_ref[...], w_ref[...], preferred_element_type=F32) + b_ref[...]
    gn = gn_ref[...].astype(F32)
    o_ref[...] = ((gn * _sigmoid(z)) * _silu(gate_ref[...].astype(F32))).astype(o_ref.dtype)


def s5_glu(gact, w_glu, b_glu, ug, mixed, col0, tm=512, tn=512):
    t, width = gact.shape
    tm, tn = _pick(t, tm), _pick(width, tn)
    while col0 % tn:
        tn -= LANES
    nj = width // tn
    return pl.pallas_call(
        _glu_kernel,
        out_shape=jax.ShapeDtypeStruct(mixed.shape, mixed.dtype),
        grid=(t // tm, nj),
        in_specs=[pl.BlockSpec((tm, width), lambda i, j: (i, 0)),
                  pl.BlockSpec((width, tn), lambda i, j: (0, j)),
                  pl.BlockSpec((1, tn), lambda i, j: (0, j)),
                  pl.BlockSpec((tm, tn), lambda i, j: (i, j)),
                  pl.BlockSpec((tm, tn), lambda i, j: (i, nj + j)),
                  pl.BlockSpec(memory_space=pl.ANY)],
        out_specs=pl.BlockSpec((tm, tn), lambda i, j: (i, col0 // tn + j)),
        input_output_aliases={5: 0},
        compiler_params=_cparams(("parallel", "arbitrary")),
        name="s5_glu",
    )(gact, w_glu.astype(BF16), b_glu.astype(F32).reshape(1, width), gact, ug, mixed)


def _conv_kernel(cur_ref, tail_ref, w_ref, b_ref, o_ref, *, halo):
    c = pl.program_id(1)
    cur = cur_ref[...].astype(F32)
    tail = jnp.where(c > 0, tail_ref[...].astype(F32), 0.0)
    ext = jnp.concatenate([tail, cur], axis=0)
    kw = w_ref.shape[0]
    acc = b_ref[...] + w_ref[kw - 1:kw, :] * cur
    for k in range(kw - 1):
        acc = acc + w_ref[k:k + 1, :] * pltpu.roll(ext, kw - 1 - k, axis=0)[halo:, :]
    o_ref[...] = _silu(acc).astype(o_ref.dtype)


def ssd_conv(xbc, conv_w, conv_b, bsz, seqlen, tc=512, wc=1024):
    t, ch = xbc.shape
    tc, wc = _pick(seqlen, tc), _pick(ch, wc)
    nc = seqlen // tc
    halo = 16
    hb = tc // halo
    return pl.pallas_call(
        functools.partial(_conv_kernel, halo=halo),
        out_shape=jax.ShapeDtypeStruct((t, ch), BF16),
        grid=(bsz, nc, ch // wc),
        in_specs=[pl.BlockSpec((tc, wc), lambda b, c, w: (b * nc + c, w)),
                  pl.BlockSpec((halo, wc), lambda b, c, w: (jnp.maximum((b * nc + c) * hb - 1, 0), w)),
                  pl.BlockSpec((conv_w.shape[0], wc), lambda b, c, w: (0, w)),
                  pl.BlockSpec((1, wc), lambda b, c, w: (0, w))],
        out_specs=pl.BlockSpec((tc, wc), lambda b, c, w: (b * nc + c, w)),
        compiler_params=_cparams(("parallel", "parallel", "parallel")),
        name="ssd_conv",
    )(xbc, xbc, conv_w.astype(F32), conv_b.astype(F32).reshape(1, ch))


def _expand(v, e):
    hi = v.astype(BF16)
    lo = (v - hi.astype(F32)).astype(BF16)
    return (jnp.dot(hi, e, preferred_element_type=F32)
            + jnp.dot(lo, e, preferred_element_type=F32))


def _ssd_kernel(x_ref, b_ref, c_ref, z_ref, dt_ref, dtb_ref, alog_ref, e_ref, sel_ref,
                d_ref, nw_ref, tri_ref, o_ref, st_ref, *, hpg, hd):
    ck = pl.program_id(2)

    @pl.when(ck == 0)
    def _():
        st_ref[...] = jnp.zeros_like(st_ref)

    q = x_ref.shape[0]
    e = e_ref[...]
    dt = _softplus(dt_ref[...] + dtb_ref[...])
    la = dt * (-jnp.exp(alog_ref[...]))
    cum = jnp.dot(tri_ref[...], la, preferred_element_type=F32, precision=lax.Precision.HIGHEST)
    cum_last = cum[q - 1:q, :]
    dec_end = jnp.exp(cum_last - cum)
    dec_out = jnp.exp(cum)
    chunk_dec = jnp.exp(jnp.broadcast_to(cum_last, (SUBLANES, LANES)))

    x = x_ref[...].astype(F32)
    xdt = x * _expand(dt, e)
    xdt_b = xdt.astype(BF16)
    xde = (xdt * _expand(dec_end, e)).astype(BF16)
    bm = b_ref[...]
    cm = c_ref[...]
    scores = lax.dot_general(cm, bm, (((1,), (1,)), ((), ())), preferred_element_type=F32)

    cum_g = jnp.dot(cum, sel_ref[...], preferred_element_type=F32, precision=lax.Precision.HIGHEST)
    cum_t = cum_g.T
    causal = (lax.broadcasted_iota(jnp.int32, (q, q), 0) >= lax.broadcasted_iota(jnp.int32, (q, q), 1))
    lane = lax.broadcasted_iota(jnp.int32, (q, LANES), 1)
    hpt = LANES // hd
    parts = []
    for tl in range(hpg // hpt):
        xt = xdt_b[:, tl * LANES:(tl + 1) * LANES]
        acc = None
        for hh in range(hpt):
            r = tl * hpt + hh
            seg = cum_g[:, r:r + 1] - cum_t[r:r + 1, :]
            wgt = (jnp.where(causal, jnp.exp(seg), 0.0) * scores).astype(BF16)
            xm = jnp.where((lane >= hh * hd) & (lane < (hh + 1) * hd), xt, jnp.zeros_like(xt))
            part = jnp.dot(wgt, xm, preferred_element_type=F32)
            acc = part if acc is None else acc + part
        parts.append(acc)
    y_diag = jnp.concatenate(parts, axis=1)

    st = st_ref[...]
    y_off = jnp.dot(cm, st.astype(BF16), preferred_element_type=F32) * _expand(dec_out, e)
    upd = lax.dot_general(bm, xde, (((0,), (0,)), ((), ())), preferred_element_type=F32)
    st_ref[...] = st * _expand(chunk_dec, e)[0:1, :] + upd

    y = y_diag + y_off + d_ref[...] * x
    y = y * _silu(z_ref[...].astype(F32))
    ms = jnp.mean(y * y, axis=-1, keepdims=True)
    o_ref[...] = ((y * lax.rsqrt(ms + NORM_EPS)) * nw_ref[...]).astype(o_ref.dtype)


def ssd_mix(xa, z, dt_raw, dt_bias, a_log, d, norm_w, mix_width, bsz, seqlen, q=128):
    t, w = z.shape
    nh = dt_bias.shape[0]
    hd = w // nh
    ng = SSD_GROUPS
    hpg = nh // ng
    gw = hpg * hd
    ns = (xa.shape[1] - w) // (2 * ng)
    assert ns == LANES and nh <= LANES and LANES % hd == 0 and hpg % (LANES // hd) == 0
    q = min(q, seqlen)
    nc = seqlen // q
    pad = lambda a: jnp.zeros((1, LANES), F32).at[0, :nh].set(a.astype(F32))
    e_np = np.zeros((ng, LANES, gw), np.float32)
    sel_np = np.zeros((ng, LANES, LANES), np.float32)
    for g in range(ng):
        for r in range(hpg):
            e_np[g, g * hpg + r, r * hd:(r + 1) * hd] = 1.0
            sel_np[g, g * hpg + r, r] = 1.0
    tri = np.tril(np.ones((q, q), np.float32))
    row = lambda b, g, c: b * nc + c
    return pl.pallas_call(
        functools.partial(_ssd_kernel, hpg=hpg, hd=hd),
        out_shape=jax.ShapeDtypeStruct((t, mix_width), BF16),
        grid=(bsz, ng, nc),
        in_specs=[pl.BlockSpec((q, gw), lambda b, g, c: (row(b, g, c), g)),
                  pl.BlockSpec((q, ns), lambda b, g, c: (row(b, g, c), w // ns + g)),
                  pl.BlockSpec((q, ns), lambda b, g, c: (row(b, g, c), w // ns + ng + g)),
                  pl.BlockSpec((q, gw), lambda b, g, c: (row(b, g, c), g)),
                  pl.BlockSpec((q, LANES), lambda b, g, c: (row(b, g, c), 0)),
                  pl.BlockSpec((1, LANES), lambda b, g, c: (0, 0)),
                  pl.BlockSpec((1, LANES), lambda b, g, c: (0, 0)),
                  pl.BlockSpec((None, LANES, gw), lambda b, g, c: (g, 0, 0)),
                  pl.BlockSpec((None, LANES, LANES), lambda b, g, c: (g, 0, 0)),
                  pl.BlockSpec((1, gw), lambda b, g, c: (0, g)),
                  pl.BlockSpec((1, gw), lambda b, g, c: (0, g)),
                  pl.BlockSpec((q, q), lambda b, g, c: (0, 0))],
        out_specs=pl.BlockSpec((q, gw), lambda b, g, c: (row(b, g, c), g)),
        scratch_shapes=[pltpu.VMEM((ns, gw), F32)],
        compiler_params=_cparams(("parallel", "parallel", "arbitrary")),
        name="ssd_mix",
    )(xa, xa, xa, z, dt_raw, pad(dt_bias), pad(a_log), jnp.asarray(e_np, BF16), jnp.asarray(sel_np),
      jnp.repeat(d.astype(F32), hd).reshape(1, w), norm_w.astype(F32).reshape(1, w), jnp.asarray(tri))


def _logf_cumsum_kernel(f_ref, b_ref, tri_ref, o_ref, car_ref):
    c = pl.program_id(1)

    @pl.when(c == 0)
    def _():
        car_ref[...] = jnp.zeros_like(car_ref)

    lf = -_softplus(-(f_ref[...] + b_ref[...]))
    cum = jnp.dot(tri_ref[...], lf, preferred_element_type=F32,
                  precision=lax.Precision.HIGHEST) + car_ref[...]
    o_ref[...] = cum
    car_ref[...] = cum[cum.shape[0] - 1:, :]


def logf_cumsum(f_raw, b_f, bsz, seqlen, tc=256):
    t = f_raw.shape[0]
    nh = b_f.shape[0]
    tc = min(tc, seqlen)
    nc = seqlen // tc
    b_pad = jnp.zeros((1, LANES), F32).at[0, :nh].set(b_f.astype(F32))
    tri = np.tril(np.ones((tc, tc), np.float32))
    return pl.pallas_call(
        _logf_cumsum_kernel,
        out_shape=jax.ShapeDtypeStruct((t, LANES), F32),
        grid=(bsz, nc),
        in_specs=[pl.BlockSpec((tc, LANES), lambda b, c: (b * nc + c, 0)),
                  pl.BlockSpec((1, LANES), lambda b, c: (0, 0)),
                  pl.BlockSpec((tc, tc), lambda b, c: (0, 0))],
        out_specs=pl.BlockSpec((tc, LANES), lambda b, c: (b * nc + c, 0)),
        scratch_shapes=[pltpu.VMEM((1, LANES), F32)],
        compiler_params=_cparams(("parallel", "arbitrary")),
        name="logf_cumsum",
    )(f_raw, b_pad, jnp.asarray(tri))


def _fox_kernel(q_ref, k_ref, v_ref, cum_ref, gate_ref, o_ref, qs_ref, vt_ref, ck_ref, s_ref, *, scale, tb):
    h = pl.program_id(1)
    qi = pl.program_id(2)
    nb = vt_ref.shape[0]
    hd = q_ref.shape[1]

    @pl.when(qi == 0)
    def _():
        for j in range(nb):
            vt_ref[j] = v_ref[j * tb:(j + 1) * tb, :].T
        onehot = (lax.broadcasted_iota(jnp.int32, (LANES, LANES), 0) == h).astype(F32)
        ck_ref[...] = jnp.dot(cum_ref[...], onehot, preferred_element_type=F32,
                              precision=lax.Precision.HIGHEST) * LOG2E

    qs_ref[...] = (q_ref[...].astype(F32) * (scale * LOG2E)).astype(BF16)

    def rows(j):
        return pl.ds(pl.multiple_of(j * tb, tb), tb)

    def qk(j):
        return lax.dot_general(k_ref[rows(j), :], qs_ref[...], (((1,), (1,)), ((), ())),
                               preferred_element_type=F32)

    def process(j, masked, m, l, acc):
        c = ck_ref[rows(j), :]
        t = s_ref[j % 2] - jnp.concatenate([c] * (tb // LANES), axis=1)
        if masked:
            keep = (lax.broadcasted_iota(jnp.int32, (tb, tb), 0) <= lax.broadcasted_iota(jnp.int32, (tb, tb), 1))
            t = jnp.where(keep, t, -jnp.inf)
        m_new = jnp.maximum(m, jnp.max(t, axis=0, keepdims=True))
        alpha = jnp.exp2(m - m_new)
        p = jnp.exp2(t - m_new)
        l = alpha * l + jnp.sum(p, axis=0, keepdims=True)
        acc = alpha * acc + jnp.dot(vt_ref[j], p.astype(BF16), preferred_element_type=F32)
        return m_new, l, acc

    s_ref[0] = qk(0)

    def body(j, carry):
        out = process(j, False, *carry)
        s_ref[(j + 1) % 2] = qk(j + 1)
        return out

    init = (jnp.full((1, tb), -jnp.inf, F32), jnp.zeros((1, tb), F32), jnp.zeros((hd, tb), F32))
    m, l, acc = lax.fori_loop(0, qi, body, init)
    m, l, acc = process(qi, True, m, l, acc)
    att = (acc / l).T
    o_ref[...] = (att * _silu(gate_ref[...].astype(F32))).astype(o_ref.dtype)


def fox_attention(qkvg, cum, nh, hd, bsz, seqlen, tb=512):
    t = qkvg.shape[0]
    tb = min(tb, seqlen)
    nb = seqlen // tb
    assert hd == LANES
    return pl.pallas_call(
        functools.partial(_fox_kernel, scale=1.0 / math.sqrt(hd), tb=tb),
        out_shape=jax.ShapeDtypeStruct((t, nh * hd), BF16),
        grid=(bsz, nh, nb),
        in_specs=[pl.BlockSpec((tb, hd), lambda b, h, i: (b * nb + i, h)),
                  pl.BlockSpec((seqlen, hd), lambda b, h, i: (b, nh + h)),
                  pl.BlockSpec((seqlen, hd), lambda b, h, i: (b, 2 * nh + h)),
                  pl.BlockSpec((seqlen, LANES), lambda b, h, i: (b, 0)),
                  pl.BlockSpec((tb, hd), lambda b, h, i: (b * nb + i, 3 * nh + h))],
        out_specs=pl.BlockSpec((tb, hd), lambda b, h, i: (b * nb + i, h)),
        scratch_shapes=[pltpu.VMEM((tb, hd), BF16), pltpu.VMEM((nb, hd, tb), BF16),
                        pltpu.VMEM((seqlen, LANES), F32), pltpu.VMEM((2, tb, tb), F32)],
        compiler_params=_cparams(("parallel", "parallel", "arbitrary")),
        name="fox_attention",
    )(qkvg, qkvg, qkvg, cum, qkvg)


def _pad_cols(w, n):
    return jnp.pad(w, ((0, 0), (0, n - w.shape[1])))


def kernel(x, l0_norm_w, l0_w_in, l0_s5_lambda_re, l0_s5_lambda_im, l0_s5_log_step, l0_s5_b_re, l0_s5_b_im, l0_s5_c_re, l0_s5_c_im, l0_s5_d, l0_s5_w_glu, l0_s5_b_glu, l0_ssd_conv_w, l0_ssd_conv_b, l0_ssd_dt_bias, l0_ssd_a_log, l0_ssd_d, l0_ssd_norm_w, l0_w_out, l1_norm_w, l1_w_in, l1_fox_b_f, l1_w_out, final_norm_w):
    bsz, seqlen, dm = x.shape
    t = bsz * seqlen
    x2 = x.reshape(t, dm)

    s5_w = l0_s5_w_glu.shape[0]
    ssd_w = l0_ssd_norm_w.shape[0]
    xbc_w = l0_ssd_conv_w.shape[1]
    ssd_h = l0_ssd_dt_bias.shape[0]
    mix_w = s5_w + ssd_w
    o_z = 2 * s5_w
    o_xbc = o_z + ssd_w
    o_dt = o_xbc + xbc_w

    h0 = rmsnorm(x2, l0_norm_w, BF16)
    wcol = lambda w, a, b: w[:, a:b].astype(BF16)
    ug = matmul(h0, wcol(l0_w_in, 0, o_z), BF16)
    z = matmul(h0, wcol(l0_w_in, o_z, o_xbc), BF16)
    xbc = matmul(h0, wcol(l0_w_in, o_xbc, o_dt), BF16)
    dt_raw = matmul(h0, _pad_cols(wcol(l0_w_in, o_dt, o_dt + ssd_h), LANES), F32)

    gact = s5_scan(ug, l0_s5_lambda_re, l0_s5_lambda_im, l0_s5_log_step, l0_s5_b_re, l0_s5_b_im,
                   l0_s5_c_re, l0_s5_c_im, l0_s5_d, bsz, seqlen)
    xa = ssd_conv(xbc, l0_ssd_conv_w, l0_ssd_conv_b, bsz, seqlen)
    mixed = ssd_mix(xa, z, dt_raw, l0_ssd_dt_bias, l0_ssd_a_log, l0_ssd_d, l0_ssd_norm_w,
                    mix_w, bsz, seqlen)
    mixed = s5_glu(gact, l0_s5_w_glu, l0_s5_b_glu, ug, mixed, ssd_w)
    w_out0 = jnp.concatenate([l0_w_out[s5_w:].astype(BF16), l0_w_out[:s5_w].astype(BF16)], axis=0)
    x1 = matmul(mixed, w_out0, F32, res=x2, tm=512, tn=512)

    fox_w = l1_w_out.shape[0]
    nh = l1_fox_b_f.shape[0]
    hd = fox_w // nh
    h1 = rmsnorm(x1, l1_norm_w, BF16)
    qkvg = matmul(h1, wcol(l1_w_in, 0, 4 * fox_w), BF16)
    f_raw = matmul(h1, _pad_cols(wcol(l1_w_in, 4 * fox_w, 4 * fox_w + nh), LANES), F32)
    cum = logf_cumsum(f_raw, l1_fox_b_f, bsz, seqlen)
    att = fox_attention(qkvg, cum, nh, hd, bsz, seqlen)
    x2_out = matmul(att, l1_w_out.astype(BF16), F32, res=x1, tn=512)

    return rmsnorm(x2_out, final_norm_w, F32).reshape(bsz, seqlen, dm)
```

```python
import functools
import math

import numpy as np
import jax
import jax.numpy as jnp
from jax import lax
from jax.experimental import pallas as pl
from jax.experimental.pallas import tpu as pltpu

F32 = jnp.float32
BF16 = jnp.bfloat16

NORM_EPS = 1e-5
S5_EIG_CLIP = -1e-4
SSD_GROUPS = 8
LANES = 128
SUBLANES = 8
VMEM_LIMIT = 56 * 1024 * 1024
LOG2E = 1.4426950408889634


def _cparams(sem):
    return pltpu.CompilerParams(dimension_semantics=sem, vmem_limit_bytes=VMEM_LIMIT)


def _pick(n, pref):
    if n <= pref:
        return n
    t = pref
    while t >= LANES:
        if n % t == 0:
            return t
        t -= LANES
    return n


def _sigmoid(x):
    return 1.0 / (1.0 + jnp.exp(-x))


def _silu(x):
    return x * _sigmoid(x)


def _softplus(x):
    return jnp.maximum(x, 0.0) + jnp.log1p(jnp.exp(-jnp.abs(x)))


def _rmsnorm_kernel(x_ref, w_ref, o_ref):
    x = x_ref[...]
    ms = jnp.mean(x * x, axis=-1, keepdims=True)
    o_ref[...] = ((x * lax.rsqrt(ms + NORM_EPS)) * w_ref[...]).astype(o_ref.dtype)


def rmsnorm(x2d, w, out_dtype):
    t, d = x2d.shape
    tm = _pick(t, 256)
    return pl.pallas_call(
        _rmsnorm_kernel,
        out_shape=jax.ShapeDtypeStruct((t, d), out_dtype),
        grid=(t // tm,),
        in_specs=[pl.BlockSpec((tm, d), lambda i: (i, 0)),
                  pl.BlockSpec((1, d), lambda i: (0, 0))],
        out_specs=pl.BlockSpec((tm, d), lambda i: (i, 0)),
        compiler_params=_cparams(("parallel",)),
        name="rmsnorm",
    )(x2d, w.reshape(1, d).astype(F32))


def _mm_kernel(a_ref, b_ref, *rest, has_res, rot):
    if has_res:
        r_ref, o_ref = rest
    else:
        (o_ref,) = rest
    kd = a_ref.shape[1]
    if rot:
        acc = (jnp.dot(a_ref[:, :kd - rot], b_ref[rot:, :], preferred_element_type=F32)
               + jnp.dot(a_ref[:, kd - rot:], b_ref[:rot, :], preferred_element_type=F32))
    else:
        acc = jnp.dot(a_ref[...], b_ref[...], preferred_element_type=F32)
    if has_res:
        acc = r_ref[...] + acc
    o_ref[...] = acc.astype(o_ref.dtype)


def matmul(a, b, out_dtype, n=None, col0=0, res=None, rot=0, tm=1024, tn=1024):
    m, kd = a.shape
    n = b.shape[1] if n is None else n
    tm, tn = _pick(m, tm), _pick(n, tn)
    while col0 % tn or n % tn:
        tn -= LANES
    cb = col0 // tn
    in_specs = [pl.BlockSpec((tm, kd), lambda i, j: (i, 0)),
                pl.BlockSpec((kd, tn), lambda i, j: (0, cb + j))]
    args = [a, b]
    if res is not None:
        in_specs.append(pl.BlockSpec((tm, tn), lambda i, j: (i, j)))
        args.append(res)
    return pl.pallas_call(
        functools.partial(_mm_kernel, has_res=res is not None, rot=rot),
        out_shape=jax.ShapeDtypeStruct((m, n), out_dtype),
        grid=(m // tm, n // tn),
        in_specs=in_specs,
        out_specs=pl.BlockSpec((tm, tn), lambda i, j: (i, j)),
        compiler_params=_cparams(("parallel", "parallel")),
        name="matmul",
    )(*args)


def _s5_prep_kernel(lr_ref, li_ref, st_ref, br_ref, bi_ref, lrf_ref, lif_ref, stf_ref,
                    bbr_ref, bbi_ref, pre_ref, pim_ref):
    lr = jnp.minimum(lr_ref[...], S5_EIG_CLIP)
    li = li_ref[...]
    step = jnp.exp(st_ref[...])
    mag = jnp.exp(lr * step)
    ab_re = mag * jnp.cos(li * step)
    ab_im = mag * jnp.sin(li * step)
    denom = lr * lr + li * li
    nr = ab_re - 1.0
    ni = ab_im
    coef_re = (nr * lr + ni * li) / denom
    coef_im = (ni * lr - nr * li) / denom
    br = br_ref[...]
    bi = bi_ref[...]
    bbr_ref[...] = coef_re * br - coef_im * bi
    bbi_ref[...] = coef_re * bi + coef_im * br
    lrf = jnp.minimum(lrf_ref[...], S5_EIG_CLIP)
    stepf = jnp.exp(stf_ref[...])
    n = pre_ref.shape[0]
    kk = (lax.broadcasted_iota(jnp.int32, (n, 1), 0) + 1).astype(F32)
    magp = jnp.exp(kk * (lrf * stepf))
    ang = kk * (lif_ref[...] * stepf)
    pre_ref[...] = magp * jnp.cos(ang)
    pim_ref[...] = magp * jnp.sin(ang)


def _s5_scan_kernel(u_ref, pm_ref, pmt_ref, bre_ref, bim_ref, cre_ref, cim_ref, d_ref,
                    pre_ref, pim_ref, o_ref, car_re, car_im, s_re, s_im, sb_re, sb_im, *, n):
    c = pl.program_id(1)
    nb = u_ref.shape[0]
    bs = range(nb)

    @pl.when(c == 0)
    def _():
        car_re[...] = jnp.zeros_like(car_re)
        car_im[...] = jnp.zeros_like(car_im)

    sl = lambda i: pl.ds(i * SUBLANES, SUBLANES)
    sl2 = lambda i: pl.ds(i * 2 * SUBLANES, 2 * SUBLANES)
    up = [jnp.dot(pm_ref[...], u_ref[b], preferred_element_type=F32) for b in bs]
    for b in bs:
        upb = up[b].astype(BF16)
        s_re[b] = jnp.dot(upb, bre_ref[...], preferred_element_type=F32)
        s_im[b] = jnp.dot(upb, bim_ref[...], preferred_element_type=F32)
    w = s_re.shape[2]
    a_re = jnp.broadcast_to(pre_ref[0:1, :], (SUBLANES, w))
    a_im = jnp.broadcast_to(pim_ref[0:1, :], (SUBLANES, w))
    x_re = [jnp.zeros((SUBLANES, w), F32) for _ in bs]
    x_im = [jnp.zeros((SUBLANES, w), F32) for _ in bs]
    for i in range(n):
        for b in bs:
            n_re = a_re * x_re[b] - a_im * x_im[b] + s_re[b, sl(i), :]
            n_im = a_re * x_im[b] + a_im * x_re[b] + s_im[b, sl(i), :]
            x_re[b], x_im[b] = n_re, n_im
            s_re[b, sl(i), :] = n_re
            s_im[b, sl(i), :] = n_im
    an_re = pre_ref[n - 1:n, :]
    an_im = pim_ref[n - 1:n, :]
    cm_re, cm_im = [], []
    for b in bs:
        c_re = car_re[b]
        c_im = car_im[b]
        rows_re, rows_im = [c_re], [c_im]
        for s in range(1, SUBLANES + 1):
            e_re = x_re[b][s - 1:s, :]
            e_im = x_im[b][s - 1:s, :]
            c_re, c_im = (e_re + an_re * c_re - an_im * c_im,
                          e_im + an_re * c_im + an_im * c_re)
            if s < SUBLANES:
                rows_re.append(c_re)
                rows_im.append(c_im)
        car_re[b] = c_re
        car_im[b] = c_im
        cm_re.append(jnp.concatenate(rows_re + rows_re, axis=0))
        cm_im.append(jnp.concatenate(rows_im + rows_im, axis=0))
    for i in range(n // 2):
        p_re = jnp.concatenate([jnp.broadcast_to(pre_ref[2 * i + k:2 * i + k + 1, :], (SUBLANES, w))
                                for k in range(2)], axis=0)
        p_im = jnp.concatenate([jnp.broadcast_to(pim_ref[2 * i + k:2 * i + k + 1, :], (SUBLANES, w))
                                for k in range(2)], axis=0)
        for b in bs:
            sb_re[b, sl2(i), :] = (s_re[b, sl2(i), :] + p_re * cm_re[b] - p_im * cm_im[b]).astype(BF16)
            sb_im[b, sl2(i), :] = (s_im[b, sl2(i), :] + p_re * cm_im[b] + p_im * cm_re[b]).astype(BF16)
    for b in bs:
        yp = (jnp.dot(sb_re[b], cre_ref[...], preferred_element_type=F32)
              - jnp.dot(sb_im[b], cim_ref[...], preferred_element_type=F32)
              + d_ref[...] * up[b])
        y = jnp.dot(pmt_ref[...], yp.astype(BF16), preferred_element_type=F32)
        cdf = 0.5 * (1.0 + jnp.tanh(np.float32(math.sqrt(2.0 / math.pi)) * (y + 0.044715 * (y * y * y))))
        o_ref[b] = (y * cdf).astype(o_ref.dtype)


def _s5_perm(tc):
    n = tc // SUBLANES
    pm = np.zeros((tc, tc), np.float32)
    for i in range(n):
        for s in range(SUBLANES):
            pm[i * SUBLANES + s, s * n + i] = 1.0
    return pm


def s5_scan(ug, lam_re, lam_im, log_step, b_re, b_im, c_re, c_im, d, bsz, seqlen, tc=256):
    g, p = lam_re.shape
    hh = b_re.shape[2]
    width = g * hh
    gpt = LANES // hh
    nt = width // LANES
    sw = gpt * p
    tc = min(tc, seqlen)
    n = tc // SUBLANES
    nc = seqlen // tc
    t = bsz * seqlen

    rep = lambda a: jnp.repeat(a.astype(F32), hh, axis=1)
    lr_rep, li_rep = rep(lam_re), rep(lam_im)
    st_rep = jnp.broadcast_to(log_step.astype(F32)[:, None], (g, p * hh))
    flat = lambda a: a.astype(F32).reshape(1, g * p)
    st_flat = jnp.broadcast_to(log_step.astype(F32)[:, None], (g, p)).reshape(1, g * p)
    bb_re, bb_im, ptab_re, ptab_im = pl.pallas_call(
        _s5_prep_kernel,
        out_shape=[jax.ShapeDtypeStruct((g, p * hh), F32)] * 2
        + [jax.ShapeDtypeStruct((n, g * p), F32)] * 2,
        name="s5_prep",
    )(lr_rep, li_rep, st_rep, b_re.astype(F32).reshape(g, p * hh), b_im.astype(F32).reshape(g, p * hh),
      flat(lam_re), flat(lam_im), st_flat)

    eye = jnp.eye(gpt, dtype=F32)
    bd_b = lambda bb: jnp.einsum('jgph,gk->jghkp', bb.reshape(nt, gpt, p, hh), eye
                                 ).reshape(nt, LANES, sw).astype(BF16)
    bd_c = lambda cc: jnp.einsum('jghp,gk->jkpgh', cc.astype(F32).reshape(nt, gpt, hh, p), eye
                                 ).reshape(nt, sw, LANES).astype(BF16)
    pm = _s5_perm(tc)
    tile3 = lambda r, c: pl.BlockSpec((None, r, c), lambda j, k: (j, 0, 0))
    full = lambda r, c: pl.BlockSpec((r, c), lambda j, k: (0, 0))
    out = pl.pallas_call(
        functools.partial(_s5_scan_kernel, n=n),
        out_shape=jax.ShapeDtypeStruct((bsz, seqlen, width), BF16),
        grid=(nt, nc),
        in_specs=[pl.BlockSpec((bsz, tc, LANES), lambda j, k: (0, k, j)),
                  full(tc, tc), full(tc, tc),
                  tile3(LANES, sw), tile3(LANES, sw), tile3(sw, LANES), tile3(sw, LANES),
                  tile3(1, LANES),
                  pl.BlockSpec((n, sw), lambda j, k: (0, j)),
                  pl.BlockSpec((n, sw), lambda j, k: (0, j))],
        out_specs=pl.BlockSpec((bsz, tc, LANES), lambda j, k: (0, k, j)),
        scratch_shapes=[pltpu.VMEM((bsz, 1, sw), F32), pltpu.VMEM((bsz, 1, sw), F32),
                        pltpu.VMEM((bsz, tc, sw), F32), pltpu.VMEM((bsz, tc, sw), F32),
                        pltpu.VMEM((bsz, tc, sw), BF16), pltpu.VMEM((bsz, tc, sw), BF16)],
        compiler_params=_cparams(("parallel", "arbitrary")),
        name="s5_scan",
    )(ug.reshape(bsz, seqlen, ug.shape[1]), jnp.asarray(pm, BF16), jnp.asarray(pm.T, BF16),
      bd_b(bb_re), bd_b(bb_im), bd_c(c_re), bd_c(c_im), d.astype(F32).reshape(nt, 1, LANES),
      ptab_re, ptab_im)
    return out.reshape(t, width)


def _glu_kernel(a_ref, w_ref, b_ref, gn_ref, gate_ref, mixed_ref, o_ref):
    del mixed_ref
    z = jnp.dot(a_ref[...], w_ref[...], preferred_element_type=F32) + b_ref[...]
    gn = gn_ref[...].astype(F32)
    o_ref[...] = ((gn * _sigmoid(z)) * _silu(gate_ref[...].astype(F32))).astype(o_ref.dtype)


def s5_glu(gact, w_glu, b_glu, ug, mixed, col0, tm=512, tn=512):
    t, width = gact.shape
    tm, tn = _pick(t, tm), _pick(width, tn)
    while col0 % tn:
        tn -= LANES
    nj = width // tn
    return pl.pallas_call(
        _glu_kernel,
        out_shape=jax.ShapeDtypeStruct(mixed.shape, mixed.dtype),
        grid=(t // tm, nj),
        in_specs=[pl.BlockSpec((tm, width), lambda i, j: (i, 0)),
                  pl.BlockSpec((width, tn), lambda i, j: (0, j)),
                  pl.BlockSpec((1, tn), lambda i, j: (0, j)),
                  pl.BlockSpec((tm, tn), lambda i, j: (i, j)),
                  pl.BlockSpec((tm, tn), lambda i, j: (i, nj + j)),
                  pl.BlockSpec(memory_space=pl.ANY)],
        out_specs=pl.BlockSpec((tm, tn), lambda i, j: (i, col0 // tn + j)),
        input_output_aliases={5: 0},
        compiler_params=_cparams(("parallel", "arbitrary")),
        name="s5_glu",
    )(gact, w_glu.astype(BF16), b_glu.astype(F32).reshape(1, width), gact, ug, mixed)


def _conv_kernel(cur_ref, tail_ref, w_ref, b_ref, o_ref, *, halo):
    c = pl.program_id(1)
    cur = cur_ref[...].astype(F32)
    tail = jnp.where(c > 0, tail_ref[...].astype(F32), 0.0)
    ext = jnp.concatenate([tail, cur], axis=0)
    kw = w_ref.shape[0]
    acc = b_ref[...] + w_ref[kw - 1:kw, :] * cur
    for k in range(kw - 1):
        acc = acc + w_ref[k:k + 1, :] * pltpu.roll(ext, kw - 1 - k, axis=0)[halo:, :]
    o_ref[...] = _silu(acc).astype(o_ref.dtype)


def ssd_conv(xbc, conv_w, conv_b, bsz, seqlen, tc=512, wc=1024):
    t, ch = xbc.shape
    tc, wc = _pick(seqlen, tc), _pick(ch, wc)
    nc = seqlen // tc
    halo = 16
    hb = tc // halo
    return pl.pallas_call(
        functools.partial(_conv_kernel, halo=halo),
        out_shape=jax.ShapeDtypeStruct((t, ch), BF16),
        grid=(bsz, nc, ch // wc),
        in_specs=[pl.BlockSpec((tc, wc), lambda b, c, w: (b * nc + c, w)),
                  pl.BlockSpec((halo, wc), lambda b, c, w: (jnp.maximum((b * nc + c) * hb - 1, 0), w)),
                  pl.BlockSpec((conv_w.shape[0], wc), lambda b, c, w: (0, w)),
                  pl.BlockSpec((1, wc), lambda b, c, w: (0, w))],
        out_specs=pl.BlockSpec((tc, wc), lambda b, c, w: (b * nc + c, w)),
        compiler_params=_cparams(("parallel", "parallel", "parallel")),
        name="ssd_conv",
    )(xbc, xbc, conv_w.astype(F32), conv_b.astype(F32).reshape(1, ch))


def _ssd_kernel(x_ref, b_ref, c_ref, z_ref, dt_ref, dtb_ref, alog_ref, e_ref, sel_ref,
                d_ref, nw_ref, tri_ref, o_ref, st_ref, *, hpg, hd):
    ck = pl.program_id(1)

    @pl.when(ck == 0)
    def _():
        st_ref[...] = jnp.zeros_like(st_ref)

    nb, q = x_ref.shape[0], x_ref.shape[1]
    e2 = e_ref[...]
    causal = (lax.broadcasted_iota(jnp.int32, (q, q), 0) >= lax.broadcasted_iota(jnp.int32, (q, q), 1))
    lane = lax.broadcasted_iota(jnp.int32, (q, LANES), 1)
    hpt = LANES // hd
    for b in range(nb):
        dt = _softplus(dt_ref[b] + dtb_ref[...])
        la = dt * (-jnp.exp(alog_ref[...]))
        cum = jnp.dot(tri_ref[...], la, preferred_element_type=F32, precision=lax.Precision.HIGHEST)
        cum_last = cum[q - 1:q, :]
        v = jnp.concatenate([dt, dt * jnp.exp(cum_last - cum), jnp.exp(cum),
                             jnp.exp(jnp.broadcast_to(cum_last, (SUBLANES, LANES)))], axis=0)
        hi = v.astype(BF16)
        lo = (v - hi.astype(F32)).astype(BF16)
        ex = jnp.dot(jnp.concatenate([hi, lo], axis=1), e2, preferred_element_type=F32)
        ex_dt, ex_dtend, ex_out, ex_chunk = ex[:q], ex[q:2 * q], ex[2 * q:3 * q], ex[3 * q:3 * q + 1]

        x = x_ref[b].astype(F32)
        xdt_b = (x * ex_dt).astype(BF16)
        xde = (x * ex_dtend).astype(BF16)
        bm = b_ref[b]
        cm = c_ref[b]
        scores = lax.dot_general(cm, bm, (((1,), (1,)), ((), ())), preferred_element_type=F32)

        cum_g = jnp.dot(cum, sel_ref[...], preferred_element_type=F32, precision=lax.Precision.HIGHEST)
        cum_t = cum_g.T
        parts = []
        for tl in range(hpg // hpt):
            xt = xdt_b[:, tl * LANES:(tl + 1) * LANES]
            acc = None
            for hh in range(hpt):
                r = tl * hpt + hh
                seg = cum_g[:, r:r + 1] - cum_t[r:r + 1, :]
                wgt = (jnp.where(causal, jnp.exp(seg), 0.0) * scores).astype(BF16)
                xm = jnp.where((lane >= hh * hd) & (lane < (hh + 1) * hd), xt, jnp.zeros_like(xt))
                part = jnp.dot(wgt, xm, preferred_element_type=F32)
                acc = part if acc is None else acc + part
            parts.append(acc)
        y_diag = jnp.concatenate(parts, axis=1)

        st = st_ref[b]
        y_off = jnp.dot(cm, st.astype(BF16), preferred_element_type=F32) * ex_out
        upd = lax.dot_general(bm, xde, (((0,), (0,)), ((), ())), preferred_element_type=F32)
        st_ref[b] = st * ex_chunk + upd

        y = y_diag + y_off + d_ref[...] * x
        y = y * _silu(z_ref[b].astype(F32))
        ms = jnp.mean(y * y, axis=-1, keepdims=True)
        o_ref[b] = ((y * lax.rsqrt(ms + NORM_EPS)) * nw_ref[...]).astype(o_ref.dtype)


def ssd_mix(xa, z, dt_raw, dt_bias, a_log, d, norm_w, mix_width, bsz, seqlen, q=128):
    t, w = z.shape
    nh = dt_bias.shape[0]
    hd = w // nh
    ng = SSD_GROUPS
    hpg = nh // ng
    gw = hpg * hd
    ns = (xa.shape[1] - w) // (2 * ng)
    assert ns == LANES and nh <= LANES and LANES % hd == 0 and hpg % (LANES // hd) == 0
    q = min(q, seqlen)
    nc = seqlen // q
    pad = lambda a: jnp.zeros((1, LANES), F32).at[0, :nh].set(a.astype(F32))
    e_np = np.zeros((ng, 2 * LANES, gw), np.float32)
    sel_np = np.zeros((ng, LANES, LANES), np.float32)
    for g in range(ng):
        for r in range(hpg):
            e_np[g, g * hpg + r, r * hd:(r + 1) * hd] = 1.0
            e_np[g, LANES + g * hpg + r, r * hd:(r + 1) * hd] = 1.0
            sel_np[g, g * hpg + r, r] = 1.0
    tri = np.tril(np.ones((q, q), np.float32))
    xa3 = xa.reshape(bsz, seqlen, xa.shape[1])
    blk = lambda wd, off: pl.BlockSpec((bsz, q, wd), lambda g, c: (0, c, off + g))
    out = pl.pallas_call(
        functools.partial(_ssd_kernel, hpg=hpg, hd=hd),
        out_shape=jax.ShapeDtypeStruct((bsz, seqlen, mix_width), BF16),
        grid=(ng, nc),
        in_specs=[blk(gw, 0), blk(ns, w // ns), blk(ns, w // ns + ng), blk(gw, 0),
                  pl.BlockSpec((bsz, q, LANES), lambda g, c: (0, c, 0)),
                  pl.BlockSpec((1, LANES), lambda g, c: (0, 0)),
                  pl.BlockSpec((1, LANES), lambda g, c: (0, 0)),
                  pl.BlockSpec((None, 2 * LANES, gw), lambda g, c: (g, 0, 0)),
                  pl.BlockSpec((None, LANES, LANES), lambda g, c: (g, 0, 0)),
                  pl.BlockSpec((1, gw), lambda g, c: (0, g)),
                  pl.BlockSpec((1, gw), lambda g, c: (0, g)),
                  pl.BlockSpec((q, q), lambda g, c: (0, 0))],
        out_specs=blk(gw, 0),
        scratch_shapes=[pltpu.VMEM((bsz, ns, gw), F32)],
        compiler_params=_cparams(("parallel", "arbitrary")),
        name="ssd_mix",
    )(xa3, xa3, xa3, z.reshape(bsz, seqlen, w), dt_raw.reshape(bsz, seqlen, LANES), pad(dt_bias),
      pad(a_log), jnp.asarray(e_np, BF16), jnp.asarray(sel_np),
      jnp.repeat(d.astype(F32), hd).reshape(1, w), norm_w.astype(F32).reshape(1, w), jnp.asarray(tri))
    return out.reshape(t, mix_width)


def _logf_cumsum_kernel(f_ref, b_ref, tri_ref, o_ref, car_ref):
    c = pl.program_id(1)

    @pl.when(c == 0)
    def _():
        car_ref[...] = jnp.zeros_like(car_ref)

    lf = -_softplus(-(f_ref[...] + b_ref[...]))
    cum = jnp.dot(tri_ref[...], lf, preferred_element_type=F32,
                  precision=lax.Precision.HIGHEST) + car_ref[...]
    o_ref[...] = cum
    car_ref[...] = cum[cum.shape[0] - 1:, :]


def logf_cumsum(f_raw, b_f, bsz, seqlen, tc=256):
    t = f_raw.shape[0]
    nh = b_f.shape[0]
    tc = min(tc, seqlen)
    nc = seqlen // tc
    b_pad = jnp.zeros((1, LANES), F32).at[0, :nh].set(b_f.astype(F32))
    tri = np.tril(np.ones((tc, tc), np.float32))
    return pl.pallas_call(
        _logf_cumsum_kernel,
        out_shape=jax.ShapeDtypeStruct((t, LANES), F32),
        grid=(bsz, nc),
        in_specs=[pl.BlockSpec((tc, LANES), lambda b, c: (b * nc + c, 0)),
                  pl.BlockSpec((1, LANES), lambda b, c: (0, 0)),
                  pl.BlockSpec((tc, tc), lambda b, c: (0, 0))],
        out_specs=pl.BlockSpec((tc, LANES), lambda b, c: (b * nc + c, 0)),
        scratch_shapes=[pltpu.VMEM((1, LANES), F32)],
        compiler_params=_cparams(("parallel", "arbitrary")),
        name="logf_cumsum",
    )(f_raw, b_pad, jnp.asarray(tri))


def _fox_kernel(q_ref, k_ref, v_ref, cum_ref, gate_ref, o_ref, qs_ref, vt_ref, ck_ref, s_ref, p_ref,
                m_ref, l_ref, al_ref, acc_ref, *, scale, tb):
    h = pl.program_id(1)
    qi = pl.program_id(2)
    nb = vt_ref.shape[0]
    hd = q_ref.shape[1]

    @pl.when(qi == 0)
    def _():
        for j in range(nb):
            vt_ref[j] = v_ref[j * tb:(j + 1) * tb, :].T
        onehot = (lax.broadcasted_iota(jnp.int32, (LANES, LANES), 0) == h).astype(F32)
        ck_ref[...] = jnp.dot(cum_ref[...], onehot, preferred_element_type=F32,
                              precision=lax.Precision.HIGHEST) * LOG2E

    qs_ref[...] = (q_ref[...].astype(F32) * (scale * LOG2E)).astype(BF16)

    def rows(j):
        return pl.ds(pl.multiple_of(j * tb, tb), tb)

    def qk(j):
        return lax.dot_general(k_ref[rows(j), :], qs_ref[...], (((1,), (1,)), ((), ())),
                               preferred_element_type=F32)

    def softmax_stage(j, masked):
        c = ck_ref[rows(j), :]
        t = s_ref[j % 2] - jnp.concatenate([c] * (tb // LANES), axis=1)
        if masked:
            keep = (lax.broadcasted_iota(jnp.int32, (tb, tb), 0) <= lax.broadcasted_iota(jnp.int32, (tb, tb), 1))
            t = jnp.where(keep, t, -jnp.inf)
        m = m_ref[...]
        m_new = jnp.maximum(m, jnp.max(t, axis=0, keepdims=True))
        alpha = jnp.exp2(m - m_new)
        p = jnp.exp2(t - m_new)
        l_ref[...] = alpha * l_ref[...] + jnp.sum(p, axis=0, keepdims=True)
        m_ref[...] = m_new
        al_ref[...] = alpha
        p_ref[j % 2] = p.astype(BF16)

    def pv_stage(j):
        acc_ref[...] = al_ref[...] * acc_ref[...] + jnp.dot(vt_ref[j], p_ref[j % 2],
                                                            preferred_element_type=F32)

    m_ref[...] = jnp.full_like(m_ref, -jnp.inf)
    l_ref[...] = jnp.zeros_like(l_ref)
    acc_ref[...] = jnp.zeros_like(acc_ref)
    s_ref[0] = qk(0)

    @pl.when(qi > 0)
    def _():
        softmax_stage(0, False)
        s_ref[1] = qk(1)

        def body(j, carry):
            pv_stage(j - 1)
            softmax_stage(j, False)
            s_ref[(j + 1) % 2] = qk(j + 1)
            return carry

        lax.fori_loop(1, qi, body, 0)
        pv_stage(qi - 1)
        softmax_stage(qi, True)

    @pl.when(qi == 0)
    def _():
        softmax_stage(0, True)

    pv_stage(qi)
    att = (acc_ref[...] / l_ref[...]).T
    o_ref[...] = (att * _silu(gate_ref[...].astype(F32))).astype(o_ref.dtype)


def fox_attention(qkvg, cum, nh, hd, bsz, seqlen, tb=1024):
    t = qkvg.shape[0]
    tb = min(tb, seqlen)
    nb = seqlen // tb
    assert hd == LANES
    return pl.pallas_call(
        functools.partial(_fox_kernel, scale=1.0 / math.sqrt(hd), tb=tb),
        out_shape=jax.ShapeDtypeStruct((t, nh * hd), BF16),
        grid=(bsz, nh, nb),
        in_specs=[pl.BlockSpec((tb, hd), lambda b, h, i: (b * nb + i, h)),
                  pl.BlockSpec((seqlen, hd), lambda b, h, i: (b, nh + h)),
                  pl.BlockSpec((seqlen, hd), lambda b, h, i: (b, 2 * nh + h)),
                  pl.BlockSpec((seqlen, LANES), lambda b, h, i: (b, 0)),
                  pl.BlockSpec((tb, hd), lambda b, h, i: (b * nb + i, 3 * nh + h))],
        out_specs=pl.BlockSpec((tb, hd), lambda b, h, i: (b * nb + i, h)),
        scratch_shapes=[pltpu.VMEM((tb, hd), BF16), pltpu.VMEM((nb, hd, tb), BF16),
                        pltpu.VMEM((seqlen, LANES), F32), pltpu.VMEM((2, tb, tb), F32),
                        pltpu.VMEM((2, tb, tb), BF16), pltpu.VMEM((1, tb), F32), pltpu.VMEM((1, tb), F32),
                        pltpu.VMEM((1, tb), F32), pltpu.VMEM((hd, tb), F32)],
        compiler_params=_cparams(("parallel", "parallel", "arbitrary")),
        name="fox_attention",
    )(qkvg, qkvg, qkvg, cum, qkvg)


def _pad_cols(w, n):
    return jnp.pad(w, ((0, 0), (0, n - w.shape[1])))


def kernel(x, l0_norm_w, l0_w_in, l0_s5_lambda_re, l0_s5_lambda_im, l0_s5_log_step, l0_s5_b_re, l0_s5_b_im, l0_s5_c_re, l0_s5_c_im, l0_s5_d, l0_s5_w_glu, l0_s5_b_glu, l0_ssd_conv_w, l0_ssd_conv_b, l0_ssd_dt_bias, l0_ssd_a_log, l0_ssd_d, l0_ssd_norm_w, l0_w_out, l1_norm_w, l1_w_in, l1_fox_b_f, l1_w_out, final_norm_w):
    bsz, seqlen, dm = x.shape
    t = bsz * seqlen
    x2 = x.reshape(t, dm)

    s5_w = l0_s5_w_glu.shape[0]
    ssd_w = l0_ssd_norm_w.shape[0]
    xbc_w = l0_ssd_conv_w.shape[1]
    ssd_h = l0_ssd_dt_bias.shape[0]
    mix_w = s5_w + ssd_w
    o_z = 2 * s5_w
    o_xbc = o_z + ssd_w
    o_dt = o_xbc + xbc_w

    h0 = rmsnorm(x2, l0_norm_w, BF16)
    w0 = l0_w_in.astype(BF16)
    ug = matmul(h0, w0, BF16, n=o_z)
    z = matmul(h0, w0, BF16, n=ssd_w, col0=o_z)
    xbc = matmul(h0, w0, BF16, n=xbc_w, col0=o_xbc)
    dt_raw = matmul(h0, _pad_cols(w0[:, o_dt:o_dt + ssd_h], LANES), F32)

    gact = s5_scan(ug, l0_s5_lambda_re, l0_s5_lambda_im, l0_s5_log_step, l0_s5_b_re, l0_s5_b_im,
                   l0_s5_c_re, l0_s5_c_im, l0_s5_d, bsz, seqlen)
    xa = ssd_conv(xbc, l0_ssd_conv_w, l0_ssd_conv_b, bsz, seqlen)
    mixed = ssd_mix(xa, z, dt_raw, l0_ssd_dt_bias, l0_ssd_a_log, l0_ssd_d, l0_ssd_norm_w,
                    mix_w, bsz, seqlen)
    mixed = s5_glu(gact, l0_s5_w_glu, l0_s5_b_glu, ug, mixed, ssd_w)
    x1 = matmul(mixed, l0_w_out.astype(BF16), F32, res=x2, rot=s5_w, tm=512, tn=512)

    fox_w = l1_w_out.shape[0]
    nh = l1_fox_b_f.shape[0]
    hd = fox_w // nh
    h1 = rmsnorm(x1, l1_norm_w, BF16)
    w1 = l1_w_in.astype(BF16)
    qkvg = matmul(h1, w1, BF16, n=4 * fox_w)
    f_raw = matmul(h1, _pad_cols(w1[:, 4 * fox_w:4 * fox_w + nh], LANES), F32)
    cum = logf_cumsum(f_raw, l1_fox_b_f, bsz, seqlen)
    att = fox_attention(qkvg, cum, nh, hd, bsz, seqlen)
    x2_out = matmul(att, l1_w_out.astype(BF16), F32, res=x1, tn=512)

    return rmsnorm(x2_out, final_norm_w, F32).reshape(bsz, seqlen, dm)
```

```python
import functools
import math

import numpy as np
import jax
import jax.numpy as jnp
from jax import lax
from jax.experimental import pallas as pl
from jax.experimental.pallas import tpu as pltpu

F32 = jnp.float32
BF16 = jnp.bfloat16

NORM_EPS = 1e-5
S5_EIG_CLIP = -1e-4
SSD_GROUPS = 8
LANES = 128
SUBLANES = 8
VMEM_LIMIT = 56 * 1024 * 1024
LOG2E = 1.4426950408889634


def _cparams(sem):
    return pltpu.CompilerParams(dimension_semantics=sem, vmem_limit_bytes=VMEM_LIMIT)


def _pick(n, pref):
    if n <= pref:
        return n
    t = pref
    while t >= LANES:
        if n % t == 0:
            return t
        t -= LANES
    return n


def _sigmoid(x):
    return 1.0 / (1.0 + jnp.exp(-x))


def _silu(x):
    return x * _sigmoid(x)


def _softplus(x):
    return jnp.maximum(x, 0.0) + jnp.log1p(jnp.exp(-jnp.abs(x)))


def _rmsnorm_kernel(x_ref, w_ref, o_ref):
    x = x_ref[...]
    ms = jnp.mean(x * x, axis=-1, keepdims=True)
    o_ref[...] = ((x * lax.rsqrt(ms + NORM_EPS)) * w_ref[...]).astype(o_ref.dtype)


def rmsnorm(x2d, w, out_dtype):
    t, d = x2d.shape
    tm = _pick(t, 256)
    return pl.pallas_call(
        _rmsnorm_kernel,
        out_shape=jax.ShapeDtypeStruct((t, d), out_dtype),
        grid=(t // tm,),
        in_specs=[pl.BlockSpec((tm, d), lambda i: (i, 0)),
                  pl.BlockSpec((1, d), lambda i: (0, 0))],
        out_specs=pl.BlockSpec((tm, d), lambda i: (i, 0)),
        compiler_params=_cparams(("parallel",)),
        name="rmsnorm",
    )(x2d, w.reshape(1, d).astype(F32))


def _cast_kernel(w_ref, o_ref):
    o_ref[...] = w_ref[...].astype(o_ref.dtype)


def cast_bf16(w, ncols=None, tr=512, tc=2048):
    rows, cols = w.shape
    ncols = cols if ncols is None else ncols
    tr, tc = _pick(rows, tr), _pick(ncols, tc)
    return pl.pallas_call(
        _cast_kernel,
        out_shape=jax.ShapeDtypeStruct((rows, ncols), BF16),
        grid=(rows // tr, ncols // tc),
        in_specs=[pl.BlockSpec((tr, tc), lambda i, j: (i, j))],
        out_specs=pl.BlockSpec((tr, tc), lambda i, j: (i, j)),
        compiler_params=_cparams(("parallel", "parallel")),
        name="cast_bf16",
    )(w)


def _mm_kernel(a_ref, b_ref, *rest, has_res, rot):
    if has_res:
        r_ref, o_ref = rest
    else:
        (o_ref,) = rest
    kd = a_ref.shape[1]
    if rot:
        acc = (jnp.dot(a_ref[:, :kd - rot], b_ref[rot:, :], preferred_element_type=F32)
               + jnp.dot(a_ref[:, kd - rot:], b_ref[:rot, :], preferred_element_type=F32))
    else:
        acc = jnp.dot(a_ref[...], b_ref[...], preferred_element_type=F32)
    if has_res:
        acc = r_ref[...] + acc
    o_ref[...] = acc.astype(o_ref.dtype)


def matmul(a, b, out_dtype, n=None, col0=0, res=None, rot=0, tm=1024, tn=1024):
    m, kd = a.shape
    n = b.shape[1] if n is None else n
    tm, tn = _pick(m, tm), _pick(n, tn)
    while col0 % tn or n % tn:
        tn -= LANES
    cb = col0 // tn
    in_specs = [pl.BlockSpec((tm, kd), lambda i, j: (i, 0)),
                pl.BlockSpec((kd, tn), lambda i, j: (0, cb + j))]
    args = [a, b]
    if res is not None:
        in_specs.append(pl.BlockSpec((tm, tn), lambda i, j: (i, j)))
        args.append(res)
    return pl.pallas_call(
        functools.partial(_mm_kernel, has_res=res is not None, rot=rot),
        out_shape=jax.ShapeDtypeStruct((m, n), out_dtype),
        grid=(m // tm, n // tn),
        in_specs=in_specs,
        out_specs=pl.BlockSpec((tm, tn), lambda i, j: (i, j)),
        compiler_params=_cparams(("parallel", "parallel")),
        name="matmul",
    )(*args)


def _s5_prep_kernel(lr_ref, li_ref, st_ref, br_ref, bi_ref, lrf_ref, lif_ref, stf_ref,
                    bbr_ref, bbi_ref, pre_ref, pim_ref):
    lr = jnp.minimum(lr_ref[...], S5_EIG_CLIP)
    li = li_ref[...]
    step = jnp.exp(st_ref[...])
    mag = jnp.exp(lr * step)
    ab_re = mag * jnp.cos(li * step)
    ab_im = mag * jnp.sin(li * step)
    denom = lr * lr + li * li
    nr = ab_re - 1.0
    ni = ab_im
    coef_re = (nr * lr + ni * li) / denom
    coef_im = (ni * lr - nr * li) / denom
    br = br_ref[...]
    bi = bi_ref[...]
    bbr_ref[...] = coef_re * br - coef_im * bi
    bbi_ref[...] = coef_re * bi + coef_im * br
    lrf = jnp.minimum(lrf_ref[...], S5_EIG_CLIP)
    stepf = jnp.exp(stf_ref[...])
    n = pre_ref.shape[0]
    kk = (lax.broadcasted_iota(jnp.int32, (n, 1), 0) + 1).astype(F32)
    magp = jnp.exp(kk * (lrf * stepf))
    ang = kk * (lif_ref[...] * stepf)
    pre_ref[...] = magp * jnp.cos(ang)
    pim_ref[...] = magp * jnp.sin(ang)


def _s5_scan_kernel(u_ref, pm_ref, pmt_ref, bre_ref, bim_ref, cre_ref, cim_ref, d_ref,
                    pre_ref, pim_ref, o_ref, car_re, car_im, s_re, s_im, sb_re, sb_im, *, n):
    c = pl.program_id(1)
    nb = u_ref.shape[0]
    tps = bre_ref.shape[0]
    w = bre_ref.shape[2]
    chains = [(tt, b) for tt in range(tps) for b in range(nb)]
    ids = range(len(chains))
    lanes = lambda tt: slice(tt * LANES, (tt + 1) * LANES)
    states = lambda tt: slice(tt * w, (tt + 1) * w)

    @pl.when(c == 0)
    def _():
        car_re[...] = jnp.zeros_like(car_re)
        car_im[...] = jnp.zeros_like(car_im)

    sl = lambda i: pl.ds(i * SUBLANES, SUBLANES)
    sl2 = lambda i: pl.ds(i * 2 * SUBLANES, 2 * SUBLANES)
    up = [jnp.dot(pm_ref[...], u_ref[b, :, lanes(tt)], preferred_element_type=F32)
          for tt, b in chains]
    for k, (tt, b) in enumerate(chains):
        upb = up[k].astype(BF16)
        s_re[k] = jnp.dot(upb, bre_ref[tt], preferred_element_type=F32)
        s_im[k] = jnp.dot(upb, bim_ref[tt], preferred_element_type=F32)
    a_re = [jnp.broadcast_to(pre_ref[0:1, states(tt)], (SUBLANES, w)) for tt in range(tps)]
    a_im = [jnp.broadcast_to(pim_ref[0:1, states(tt)], (SUBLANES, w)) for tt in range(tps)]
    x_re = [jnp.zeros((SUBLANES, w), F32) for _ in ids]
    x_im = [jnp.zeros((SUBLANES, w), F32) for _ in ids]
    for i in range(n):
        for k, (tt, b) in enumerate(chains):
            n_re = a_re[tt] * x_re[k] - a_im[tt] * x_im[k] + s_re[k, sl(i), :]
            n_im = a_re[tt] * x_im[k] + a_im[tt] * x_re[k] + s_im[k, sl(i), :]
            x_re[k], x_im[k] = n_re, n_im
            s_re[k, sl(i), :] = n_re
            s_im[k, sl(i), :] = n_im
    cm_re, cm_im = [], []
    for k, (tt, b) in enumerate(chains):
        an_re = pre_ref[n - 1:n, states(tt)]
        an_im = pim_ref[n - 1:n, states(tt)]
        c_re = car_re[k]
        c_im = car_im[k]
        rows_re, rows_im = [c_re], [c_im]
        for s in range(1, SUBLANES + 1):
            e_re = x_re[k][s - 1:s, :]
            e_im = x_im[k][s - 1:s, :]
            c_re, c_im = (e_re + an_re * c_re - an_im * c_im,
                          e_im + an_re * c_im + an_im * c_re)
            if s < SUBLANES:
                rows_re.append(c_re)
                rows_im.append(c_im)
        car_re[k] = c_re
        car_im[k] = c_im
        cm_re.append(jnp.concatenate(rows_re + rows_re, axis=0))
        cm_im.append(jnp.concatenate(rows_im + rows_im, axis=0))
    for i in range(n // 2):
        p_re = [jnp.concatenate([jnp.broadcast_to(pre_ref[2 * i + r:2 * i + r + 1, states(tt)], (SUBLANES, w))
                                 for r in range(2)], axis=0) for tt in range(tps)]
        p_im = [jnp.concatenate([jnp.broadcast_to(pim_ref[2 * i + r:2 * i + r + 1, states(tt)], (SUBLANES, w))
                                 for r in range(2)], axis=0) for tt in range(tps)]
        for k, (tt, b) in enumerate(chains):
            sb_re[k, sl2(i), :] = (s_re[k, sl2(i), :] + p_re[tt] * cm_re[k] - p_im[tt] * cm_im[k]).astype(BF16)
            sb_im[k, sl2(i), :] = (s_im[k, sl2(i), :] + p_re[tt] * cm_im[k] + p_im[tt] * cm_re[k]).astype(BF16)
    for k, (tt, b) in enumerate(chains):
        yp = (jnp.dot(sb_re[k], cre_ref[tt], preferred_element_type=F32)
              - jnp.dot(sb_im[k], cim_ref[tt], preferred_element_type=F32)
              + d_ref[tt] * up[k])
        y = jnp.dot(pmt_ref[...], yp.astype(BF16), preferred_element_type=F32)
        cdf = 0.5 * (1.0 + jnp.tanh(np.float32(math.sqrt(2.0 / math.pi)) * (y + 0.044715 * (y * y * y))))
        o_ref[b, :, lanes(tt)] = (y * cdf).astype(o_ref.dtype)


def _s5_perm(tc):
    n = tc // SUBLANES
    pm = np.zeros((tc, tc), np.float32)
    for i in range(n):
        for s in range(SUBLANES):
            pm[i * SUBLANES + s, s * n + i] = 1.0
    return pm


def s5_scan(ug, lam_re, lam_im, log_step, b_re, b_im, c_re, c_im, d, bsz, seqlen, tc=256):
    g, p = lam_re.shape
    hh = b_re.shape[2]
    width = g * hh
    gpt = LANES // hh
    nt = width // LANES
    sw = gpt * p
    tc = min(tc, seqlen)
    n = tc // SUBLANES
    nc = seqlen // tc
    t = bsz * seqlen

    rep = lambda a: jnp.repeat(a.astype(F32), hh, axis=1)
    lr_rep, li_rep = rep(lam_re), rep(lam_im)
    st_rep = jnp.broadcast_to(log_step.astype(F32)[:, None], (g, p * hh))
    flat = lambda a: a.astype(F32).reshape(1, g * p)
    st_flat = jnp.broadcast_to(log_step.astype(F32)[:, None], (g, p)).reshape(1, g * p)
    bb_re, bb_im, ptab_re, ptab_im = pl.pallas_call(
        _s5_prep_kernel,
        out_shape=[jax.ShapeDtypeStruct((g, p * hh), F32)] * 2
        + [jax.ShapeDtypeStruct((n, g * p), F32)] * 2,
        name="s5_prep",
    )(lr_rep, li_rep, st_rep, b_re.astype(F32).reshape(g, p * hh), b_im.astype(F32).reshape(g, p * hh),
      flat(lam_re), flat(lam_im), st_flat)

    eye = jnp.eye(gpt, dtype=F32)
    bd_b = lambda bb: jnp.einsum('jgph,gk->jghkp', bb.reshape(nt, gpt, p, hh), eye
                                 ).reshape(nt, LANES, sw).astype(BF16)
    bd_c = lambda cc: jnp.einsum('jghp,gk->jkpgh', cc.astype(F32).reshape(nt, gpt, hh, p), eye
                                 ).reshape(nt, sw, LANES).astype(BF16)
    pm = _s5_perm(tc)
    tps = 2 if nt % 2 == 0 else 1
    nch = tps * bsz
    tile3 = lambda r, c: pl.BlockSpec((tps, r, c), lambda j, k: (j, 0, 0))
    full = lambda r, c: pl.BlockSpec((r, c), lambda j, k: (0, 0))
    out = pl.pallas_call(
        functools.partial(_s5_scan_kernel, n=n),
        out_shape=jax.ShapeDtypeStruct((bsz, seqlen, width), BF16),
        grid=(nt // tps, nc),
        in_specs=[pl.BlockSpec((bsz, tc, tps * LANES), lambda j, k: (0, k, j)),
                  full(tc, tc), full(tc, tc),
                  tile3(LANES, sw), tile3(LANES, sw), tile3(sw, LANES), tile3(sw, LANES),
                  tile3(1, LANES),
                  pl.BlockSpec((n, tps * sw), lambda j, k: (0, j)),
                  pl.BlockSpec((n, tps * sw), lambda j, k: (0, j))],
        out_specs=pl.BlockSpec((bsz, tc, tps * LANES), lambda j, k: (0, k, j)),
        scratch_shapes=[pltpu.VMEM((nch, 1, sw), F32), pltpu.VMEM((nch, 1, sw), F32),
                        pltpu.VMEM((nch, tc, sw), F32), pltpu.VMEM((nch, tc, sw), F32),
                        pltpu.VMEM((nch, tc, sw), BF16), pltpu.VMEM((nch, tc, sw), BF16)],
        compiler_params=_cparams(("parallel", "arbitrary")),
        name="s5_scan",
    )(ug.reshape(bsz, seqlen, ug.shape[1]), jnp.asarray(pm, BF16), jnp.asarray(pm.T, BF16),
      bd_b(bb_re), bd_b(bb_im), bd_c(c_re), bd_c(c_im), d.astype(F32).reshape(nt, 1, LANES),
      ptab_re, ptab_im)
    return out.reshape(t, width)


def _glu_kernel(a_ref, w_ref, b_ref, gn_ref, gate_ref, mixed_ref, o_ref):
    del mixed_ref
    z = jnp.dot(a_ref[...], w_ref[...], preferred_element_type=F32) + b_ref[...]
    gn = gn_ref[...].astype(F32)
    o_ref[...] = ((gn * _sigmoid(z)) * _silu(gate_ref[...].astype(F32))).astype(o_ref.dtype)


def s5_glu(gact, w_glu, b_glu, ug, mixed, col0, tm=512, tn=512):
    t, width = gact.shape
    tm, tn = _pick(t, tm), _pick(width, tn)
    while col0 % tn:
        tn -= LANES
    nj = width // tn
    return pl.pallas_call(
        _glu_kernel,
        out_shape=jax.ShapeDtypeStruct(mixed.shape, mixed.dtype),
        grid=(t // tm, nj),
        in_specs=[pl.BlockSpec((tm, width), lambda i, j: (i, 0)),
                  pl.BlockSpec((width, tn), lambda i, j: (0, j)),
                  pl.BlockSpec((1, tn), lambda i, j: (0, j)),
                  pl.BlockSpec((tm, tn), lambda i, j: (i, j)),
                  pl.BlockSpec((tm, tn), lambda i, j: (i, nj + j)),
                  pl.BlockSpec(memory_space=pl.ANY)],
        out_specs=pl.BlockSpec((tm, tn), lambda i, j: (i, col0 // tn + j)),
        input_output_aliases={5: 0},
        compiler_params=_cparams(("parallel", "arbitrary")),
        name="s5_glu",
    )(gact, cast_bf16(w_glu), b_glu.astype(F32).reshape(1, width), gact, ug, mixed)


def _conv_kernel(cur_ref, tail_ref, w_ref, b_ref, o_ref, *, halo):
    c = pl.program_id(1)
    cur = cur_ref[...].astype(F32)
    tail = jnp.where(c > 0, tail_ref[...].astype(F32), 0.0)
    ext = jnp.concatenate([tail, cur], axis=0)
    kw = w_ref.shape[0]
    acc = b_ref[...] + w_ref[kw - 1:kw, :] * cur
    for k in range(kw - 1):
        acc = acc + w_ref[k:k + 1, :] * pltpu.roll(ext, kw - 1 - k, axis=0)[halo:, :]
    o_ref[...] = _silu(acc).astype(o_ref.dtype)


def ssd_conv(xbc, conv_w, conv_b, bsz, seqlen, tc=512, wc=1024):
    t, ch = xbc.shape
    tc, wc = _pick(seqlen, tc), _pick(ch, wc)
    nc = seqlen // tc
    halo = 16
    hb = tc // halo
    return pl.pallas_call(
        functools.partial(_conv_kernel, halo=halo),
        out_shape=jax.ShapeDtypeStruct((t, ch), BF16),
        grid=(bsz, nc, ch // wc),
        in_specs=[pl.BlockSpec((tc, wc), lambda b, c, w: (b * nc + c, w)),
                  pl.BlockSpec((halo, wc), lambda b, c, w: (jnp.maximum((b * nc + c) * hb - 1, 0), w)),
                  pl.BlockSpec((conv_w.shape[0], wc), lambda b, c, w: (0, w)),
                  pl.BlockSpec((1, wc), lambda b, c, w: (0, w))],
        out_specs=pl.BlockSpec((tc, wc), lambda b, c, w: (b * nc + c, w)),
        compiler_params=_cparams(("parallel", "parallel", "parallel")),
        name="ssd_conv",
    )(xbc, xbc, conv_w.astype(F32), conv_b.astype(F32).reshape(1, ch))


def _ssd_decay_kernel(dt_ref, dtb_ref, alog_ref, tri_ref, dt_o, dtend_o, dout_o, cum_o):
    q = dt_ref.shape[0]
    dt = _softplus(dt_ref[...] + dtb_ref[...])
    la = dt * (-jnp.exp(alog_ref[...]))
    cum = jnp.dot(tri_ref[...], la, preferred_element_type=F32, precision=lax.Precision.HIGHEST)
    dt_o[...] = dt
    dtend_o[...] = dt * jnp.exp(cum[q - 1:q, :] - cum)
    dout_o[...] = jnp.exp(cum)
    cum_o[...] = cum


def _ssd_kernel(x_ref, b_ref, c_ref, z_ref, dt_ref, dtend_ref, dout_ref, cum_ref, e_ref, sel_ref,
                d_ref, nw_ref, o_ref, st_ref, *, hpg, hd):
    ck = pl.program_id(1)

    @pl.when(ck == 0)
    def _():
        st_ref[...] = jnp.zeros_like(st_ref)

    nb, q = x_ref.shape[0], x_ref.shape[1]
    bs = range(nb)
    e2 = e_ref[...]
    causal = (lax.broadcasted_iota(jnp.int32, (q, q), 0) >= lax.broadcasted_iota(jnp.int32, (q, q), 1))
    lane = lax.broadcasted_iota(jnp.int32, (q, LANES), 1)
    hpt = LANES // hd

    ex = []
    for b in bs:
        dout = dout_ref[b]
        v = jnp.concatenate([dt_ref[b], dtend_ref[b], dout,
                             jnp.broadcast_to(dout[q - 1:q, :], (SUBLANES, LANES))], axis=0)
        hi = v.astype(BF16)
        lo = (v - hi.astype(F32)).astype(BF16)
        ex.append(jnp.dot(jnp.concatenate([hi, lo], axis=1), e2, preferred_element_type=F32))
    x = [x_ref[b].astype(F32) for b in bs]
    xdt_b = [(x[b] * ex[b][:q]).astype(BF16) for b in bs]
    xde = [(x[b] * ex[b][q:2 * q]).astype(BF16) for b in bs]
    scores = [lax.dot_general(c_ref[b], b_ref[b], (((1,), (1,)), ((), ())), preferred_element_type=F32)
              for b in bs]
    cum_g = [jnp.dot(cum_ref[b], sel_ref[...], preferred_element_type=F32,
                     precision=lax.Precision.HIGHEST) for b in bs]
    cum_t = [cum_g[b].T for b in bs]
    parts = [[] for _ in bs]
    for tl in range(hpg // hpt):
        acc = [None for _ in bs]
        for hh in range(hpt):
            r = tl * hpt + hh
            for b in bs:
                seg = cum_g[b][:, r:r + 1] - cum_t[b][r:r + 1, :]
                wgt = (jnp.where(causal, jnp.exp(seg), 0.0) * scores[b]).astype(BF16)
                xt = xdt_b[b][:, tl * LANES:(tl + 1) * LANES]
                xm = jnp.where((lane >= hh * hd) & (lane < (hh + 1) * hd), xt, jnp.zeros_like(xt))
                part = jnp.dot(wgt, xm, preferred_element_type=F32)
                acc[b] = part if acc[b] is None else acc[b] + part
        for b in bs:
            parts[b].append(acc[b])
    for b in bs:
        st = st_ref[b]
        y_off = jnp.dot(c_ref[b], st.astype(BF16), preferred_element_type=F32) * ex[b][2 * q:3 * q]
        upd = lax.dot_general(b_ref[b], xde[b], (((0,), (0,)), ((), ())), preferred_element_type=F32)
        st_ref[b] = st * ex[b][3 * q:3 * q + 1] + upd
        y = jnp.concatenate(parts[b], axis=1) + y_off + d_ref[...] * x[b]
        y = y * _silu(z_ref[b].astype(F32))
        ms = jnp.mean(y * y, axis=-1, keepdims=True)
        o_ref[b] = ((y * lax.rsqrt(ms + NORM_EPS)) * nw_ref[...]).astype(o_ref.dtype)


def ssd_mix(xa, z, dt_raw, dt_bias, a_log, d, norm_w, mix_width, bsz, seqlen, q=128):
    t, w = z.shape
    nh = dt_bias.shape[0]
    hd = w // nh
    ng = SSD_GROUPS
    hpg = nh // ng
    gw = hpg * hd
    ns = (xa.shape[1] - w) // (2 * ng)
    assert ns == LANES and nh <= LANES and LANES % hd == 0 and hpg % (LANES // hd) == 0
    q = min(q, seqlen)
    nc = seqlen // q
    pad = lambda a: jnp.zeros((1, LANES), F32).at[0, :nh].set(a.astype(F32))
    e_np = np.zeros((ng, 2 * LANES, gw), np.float32)
    sel_np = np.zeros((ng, LANES, LANES), np.float32)
    for g in range(ng):
        for r in range(hpg):
            e_np[g, g * hpg + r, r * hd:(r + 1) * hd] = 1.0
            e_np[g, LANES + g * hpg + r, r * hd:(r + 1) * hd] = 1.0
            sel_np[g, g * hpg + r, r] = 1.0
    tri = np.tril(np.ones((q, q), np.float32))
    rowblk = pl.BlockSpec((q, LANES), lambda i: (i, 0))
    one = pl.BlockSpec((1, LANES), lambda i: (0, 0))
    dec = pl.pallas_call(
        _ssd_decay_kernel,
        out_shape=[jax.ShapeDtypeStruct((t, LANES), F32)] * 4,
        grid=(t // q,),
        in_specs=[rowblk, one, one, pl.BlockSpec((q, q), lambda i: (0, 0))],
        out_specs=[rowblk] * 4,
        compiler_params=_cparams(("parallel",)),
        name="ssd_decay",
    )(dt_raw, pad(dt_bias), pad(a_log), jnp.asarray(tri))
    dec = [a.reshape(bsz, seqlen, LANES) for a in dec]
    xa3 = xa.reshape(bsz, seqlen, xa.shape[1])
    blk = lambda wd, off: pl.BlockSpec((bsz, q, wd), lambda g, c: (0, c, off + g))
    hblk = pl.BlockSpec((bsz, q, LANES), lambda g, c: (0, c, 0))
    out = pl.pallas_call(
        functools.partial(_ssd_kernel, hpg=hpg, hd=hd),
        out_shape=jax.ShapeDtypeStruct((bsz, seqlen, mix_width), BF16),
        grid=(ng, nc),
        in_specs=[blk(gw, 0), blk(ns, w // ns), blk(ns, w // ns + ng), blk(gw, 0),
                  hblk, hblk, hblk, hblk,
                  pl.BlockSpec((None, 2 * LANES, gw), lambda g, c: (g, 0, 0)),
                  pl.BlockSpec((None, LANES, LANES), lambda g, c: (g, 0, 0)),
                  pl.BlockSpec((1, gw), lambda g, c: (0, g)),
                  pl.BlockSpec((1, gw), lambda g, c: (0, g))],
        out_specs=blk(gw, 0),
        scratch_shapes=[pltpu.VMEM((bsz, ns, gw), F32)],
        compiler_params=_cparams(("parallel", "arbitrary")),
        name="ssd_mix",
    )(xa3, xa3, xa3, z.reshape(bsz, seqlen, w), *dec, jnp.asarray(e_np, BF16), jnp.asarray(sel_np),
      jnp.repeat(d.astype(F32), hd).reshape(1, w), norm_w.astype(F32).reshape(1, w))
    return out.reshape(t, mix_width)


def _logf_cumsum_kernel(f_ref, b_ref, tri_ref, o_ref, car_ref):
    c = pl.program_id(1)

    @pl.when(c == 0)
    def _():
        car_ref[...] = jnp.zeros_like(car_ref)

    lf = -_softplus(-(f_ref[...] + b_ref[...]))
    cum = jnp.dot(tri_ref[...], lf, preferred_element_type=F32,
                  precision=lax.Precision.HIGHEST) + car_ref[...]
    o_ref[...] = cum
    car_ref[...] = cum[cum.shape[0] - 1:, :]


def logf_cumsum(f_raw, b_f, bsz, seqlen, tc=256):
    t = f_raw.shape[0]
    nh = b_f.shape[0]
    tc = min(tc, seqlen)
    nc = seqlen // tc
    b_pad = jnp.zeros((1, LANES), F32).at[0, :nh].set(b_f.astype(F32))
    tri = np.tril(np.ones((tc, tc), np.float32))
    return pl.pallas_call(
        _logf_cumsum_kernel,
        out_shape=jax.ShapeDtypeStruct((t, LANES), F32),
        grid=(bsz, nc),
        in_specs=[pl.BlockSpec((tc, LANES), lambda b, c: (b * nc + c, 0)),
                  pl.BlockSpec((1, LANES), lambda b, c: (0, 0)),
                  pl.BlockSpec((tc, tc), lambda b, c: (0, 0))],
        out_specs=pl.BlockSpec((tc, LANES), lambda b, c: (b * nc + c, 0)),
        scratch_shapes=[pltpu.VMEM((1, LANES), F32)],
        compiler_params=_cparams(("parallel", "arbitrary")),
        name="logf_cumsum",
    )(f_raw, b_pad, jnp.asarray(tri))


def _fox_kernel(q_ref, k_ref, v_ref, cum_ref, gate_ref, o_ref, qs_ref, vt_ref, ck_ref, s_ref, p_ref,
                m_ref, l_ref, al_ref, acc_ref, *, scale, tb):
    h = pl.program_id(1)
    qi = pl.program_id(2)
    nb = vt_ref.shape[0]
    hd = q_ref.shape[1]

    @pl.when(qi == 0)
    def _():
        for j in range(nb):
            vt_ref[j] = v_ref[j * tb:(j + 1) * tb, :].T
        onehot = (lax.broadcasted_iota(jnp.int32, (LANES, LANES), 0) == h).astype(F32)
        ck_ref[...] = jnp.dot(cum_ref[...], onehot, preferred_element_type=F32,
                              precision=lax.Precision.HIGHEST) * LOG2E

    qs_ref[...] = (q_ref[...].astype(F32) * (scale * LOG2E)).astype(BF16)

    def rows(j):
        return pl.ds(pl.multiple_of(j * tb, tb), tb)

    def qk(j):
        return lax.dot_general(k_ref[rows(j), :], qs_ref[...], (((1,), (1,)), ((), ())),
                               preferred_element_type=F32)

    def softmax_stage(j, masked):
        c = ck_ref[rows(j), :]
        t = s_ref[j % 2] - jnp.concatenate([c] * (tb // LANES), axis=1)
        if masked:
            keep = (lax.broadcasted_iota(jnp.int32, (tb, tb), 0) <= lax.broadcasted_iota(jnp.int32, (tb, tb), 1))
            t = jnp.where(keep, t, -jnp.inf)
        m = m_ref[...]
        m_new = jnp.maximum(m, jnp.max(t, axis=0, keepdims=True))
        alpha = jnp.exp2(m - m_new)
        p = jnp.exp2(t - m_new)
        l_ref[...] = alpha * l_ref[...] + jnp.sum(p, axis=0, keepdims=True)
        m_ref[...] = m_new
        al_ref[...] = alpha
        p_ref[j % 2] = p.astype(BF16)

    def pv_stage(j):
        acc_ref[...] = al_ref[...] * acc_ref[...] + jnp.dot(vt_ref[j], p_ref[j % 2],
                                                            preferred_element_type=F32)

    m_ref[...] = jnp.full_like(m_ref, -jnp.inf)
    l_ref[...] = jnp.zeros_like(l_ref)
    acc_ref[...] = jnp.zeros_like(acc_ref)
    s_ref[0] = qk(0)

    @pl.when(qi > 0)
    def _():
        softmax_stage(0, False)
        s_ref[1] = qk(1)

        def body(j, carry):
            pv_stage(j - 1)
            softmax_stage(j, False)
            s_ref[(j + 1) % 2] = qk(j + 1)
            return carry

        lax.fori_loop(1, qi, body, 0)
        pv_stage(qi - 1)
        softmax_stage(qi, True)

    @pl.when(qi == 0)
    def _():
        softmax_stage(0, True)

    pv_stage(qi)
    att = (acc_ref[...] / l_ref[...]).T
    o_ref[...] = (att * _silu(gate_ref[...].astype(F32))).astype(o_ref.dtype)


def fox_attention(qkvg, cum, nh, hd, bsz, seqlen, tb=1024):
    t = qkvg.shape[0]
    tb = min(tb, seqlen)
    nb = seqlen // tb
    assert hd == LANES
    return pl.pallas_call(
        functools.partial(_fox_kernel, scale=1.0 / math.sqrt(hd), tb=tb),
        out_shape=jax.ShapeDtypeStruct((t, nh * hd), BF16),
        grid=(bsz, nh, nb),
        in_specs=[pl.BlockSpec((tb, hd), lambda b, h, i: (b * nb + i, h)),
                  pl.BlockSpec((seqlen, hd), lambda b, h, i: (b, nh + h)),
                  pl.BlockSpec((seqlen, hd), lambda b, h, i: (b, 2 * nh + h)),
                  pl.BlockSpec((seqlen, LANES), lambda b, h, i: (b, 0)),
                  pl.BlockSpec((tb, hd), lambda b, h, i: (b * nb + i, 3 * nh + h))],
        out_specs=pl.BlockSpec((tb, hd), lambda b, h, i: (b * nb + i, h)),
        scratch_shapes=[pltpu.VMEM((tb, hd), BF16), pltpu.VMEM((nb, hd, tb), BF16),
                        pltpu.VMEM((seqlen, LANES), F32), pltpu.VMEM((2, tb, tb), F32),
                        pltpu.VMEM((2, tb, tb), BF16), pltpu.VMEM((1, tb), F32), pltpu.VMEM((1, tb), F32),
                        pltpu.VMEM((1, tb), F32), pltpu.VMEM((hd, tb), F32)],
        compiler_params=_cparams(("parallel", "parallel", "arbitrary")),
        name="fox_attention",
    )(qkvg, qkvg, qkvg, cum, qkvg)


def _pad_cols(w, n):
    return jnp.pad(w, ((0, 0), (0, n - w.shape[1])))


def kernel(x, l0_norm_w, l0_w_in, l0_s5_lambda_re, l0_s5_lambda_im, l0_s5_log_step, l0_s5_b_re, l0_s5_b_im, l0_s5_c_re, l0_s5_c_im, l0_s5_d, l0_s5_w_glu, l0_s5_b_glu, l0_ssd_conv_w, l0_ssd_conv_b, l0_ssd_dt_bias, l0_ssd_a_log, l0_ssd_d, l0_ssd_norm_w, l0_w_out, l1_norm_w, l1_w_in, l1_fox_b_f, l1_w_out, final_norm_w):
    bsz, seqlen, dm = x.shape
    t = bsz * seqlen
    x2 = x.reshape(t, dm)

    s5_w = l0_s5_w_glu.shape[0]
    ssd_w = l0_ssd_norm_w.shape[0]
    xbc_w = l0_ssd_conv_w.shape[1]
    ssd_h = l0_ssd_dt_bias.shape[0]
    mix_w = s5_w + ssd_w
    o_z = 2 * s5_w
    o_xbc = o_z + ssd_w
    o_dt = o_xbc + xbc_w

    h0 = rmsnorm(x2, l0_norm_w, BF16)
    w0 = cast_bf16(l0_w_in, o_dt)
    ug = matmul(h0, w0, BF16, n=o_z)
    z = matmul(h0, w0, BF16, n=ssd_w, col0=o_z)
    xbc = matmul(h0, w0, BF16, n=xbc_w, col0=o_xbc)
    dt_raw = matmul(h0, _pad_cols(l0_w_in[:, o_dt:o_dt + ssd_h].astype(BF16), LANES), F32)

    gact = s5_scan(ug, l0_s5_lambda_re, l0_s5_lambda_im, l0_s5_log_step, l0_s5_b_re, l0_s5_b_im,
                   l0_s5_c_re, l0_s5_c_im, l0_s5_d, bsz, seqlen)
    xa = ssd_conv(xbc, l0_ssd_conv_w, l0_ssd_conv_b, bsz, seqlen)
    mixed = ssd_mix(xa, z, dt_raw, l0_ssd_dt_bias, l0_ssd_a_log, l0_ssd_d, l0_ssd_norm_w,
                    mix_w, bsz, seqlen)
    mixed = s5_glu(gact, l0_s5_w_glu, l0_s5_b_glu, ug, mixed, ssd_w)
    x1 = matmul(mixed, cast_bf16(l0_w_out), F32, res=x2, rot=s5_w, tm=512, tn=512)

    fox_w = l1_w_out.shape[0]
    nh = l1_fox_b_f.shape[0]
    hd = fox_w // nh
    h1 = rmsnorm(x1, l1_norm_w, BF16)
    w1 = cast_bf16(l1_w_in, 4 * fox_w)
    qkvg = matmul(h1, w1, BF16, n=4 * fox_w)
    f_raw = matmul(h1, _pad_cols(l1_w_in[:, 4 * fox_w:4 * fox_w + nh].astype(BF16), LANES), F32)
    cum = logf_cumsum(f_raw, l1_fox_b_f, bsz, seqlen)
    att = fox_attention(qkvg, cum, nh, hd, bsz, seqlen)
    x2_out = matmul(att, cast_bf16(l1_w_out), F32, res=x1, tn=512)

    return rmsnorm(x2_out, final_norm_w, F32).reshape(bsz, seqlen, dm)
```

```python
import functools
import math

import numpy as np
import jax
import jax.numpy as jnp
from jax import lax
from jax.experimental import pallas as pl
from jax.experimental.pallas import tpu as pltpu

F32 = jnp.float32
BF16 = jnp.bfloat16

NORM_EPS = 1e-5
S5_EIG_CLIP = -1e-4
SSD_GROUPS = 8
LANES = 128
SUBLANES = 8
VMEM_LIMIT = 56 * 1024 * 1024
LOG2E = 1.4426950408889634


def _cparams(sem):
    return pltpu.CompilerParams(dimension_semantics=sem, vmem_limit_bytes=VMEM_LIMIT)


def _pick(n, pref):
    if n <= pref:
        return n
    t = pref
    while t >= LANES:
        if n % t == 0:
            return t
        t -= LANES
    return n


def _sigmoid(x):
    return 1.0 / (1.0 + jnp.exp(-x))


def _silu(x):
    return x * _sigmoid(x)


def _softplus(x):
    return jnp.maximum(x, 0.0) + jnp.log1p(jnp.exp(-jnp.abs(x)))


def _rmsnorm_kernel(x_ref, w_ref, o_ref):
    x = x_ref[...]
    ms = jnp.mean(x * x, axis=-1, keepdims=True)
    o_ref[...] = ((x * lax.rsqrt(ms + NORM_EPS)) * w_ref[...]).astype(o_ref.dtype)


def rmsnorm(x2d, w, out_dtype):
    t, d = x2d.shape
    tm = _pick(t, 256)
    return pl.pallas_call(
        _rmsnorm_kernel,
        out_shape=jax.ShapeDtypeStruct((t, d), out_dtype),
        grid=(t // tm,),
        in_specs=[pl.BlockSpec((tm, d), lambda i: (i, 0)),
                  pl.BlockSpec((1, d), lambda i: (0, 0))],
        out_specs=pl.BlockSpec((tm, d), lambda i: (i, 0)),
        compiler_params=_cparams(("parallel",)),
        name="rmsnorm",
    )(x2d, w.reshape(1, d).astype(F32))


def _mm_kernel(a_ref, b_ref, *rest, has_res, rot):
    if has_res:
        r_ref, o_ref = rest
    else:
        (o_ref,) = rest
    kd = a_ref.shape[1]
    if rot:
        acc = (jnp.dot(a_ref[:, :kd - rot], b_ref[rot:, :], preferred_element_type=F32)
               + jnp.dot(a_ref[:, kd - rot:], b_ref[:rot, :], preferred_element_type=F32))
    else:
        acc = jnp.dot(a_ref[...], b_ref[...], preferred_element_type=F32)
    if has_res:
        acc = r_ref[...] + acc
    o_ref[...] = acc.astype(o_ref.dtype)


def matmul(a, b, out_dtype, n=None, col0=0, res=None, rot=0, tm=1024, tn=1024):
    m, kd = a.shape
    n = b.shape[1] if n is None else n
    tm, tn = _pick(m, tm), _pick(n, tn)
    while col0 % tn or n % tn:
        tn -= LANES
    cb = col0 // tn
    in_specs = [pl.BlockSpec((tm, kd), lambda i, j: (i, 0)),
                pl.BlockSpec((kd, tn), lambda i, j: (0, cb + j))]
    args = [a, b]
    if res is not None:
        in_specs.append(pl.BlockSpec((tm, tn), lambda i, j: (i, j)))
        args.append(res)
    return pl.pallas_call(
        functools.partial(_mm_kernel, has_res=res is not None, rot=rot),
        out_shape=jax.ShapeDtypeStruct((m, n), out_dtype),
        grid=(m // tm, n // tn),
        in_specs=in_specs,
        out_specs=pl.BlockSpec((tm, tn), lambda i, j: (i, j)),
        compiler_params=_cparams(("parallel", "parallel")),
        name="matmul",
    )(*args)


def _s5_prep_kernel(lr_ref, li_ref, st_ref, br_ref, bi_ref, lrf_ref, lif_ref, stf_ref,
                    bbr_ref, bbi_ref, pre_ref, pim_ref):
    lr = jnp.minimum(lr_ref[...], S5_EIG_CLIP)
    li = li_ref[...]
    step = jnp.exp(st_ref[...])
    mag = jnp.exp(lr * step)
    ab_re = mag * jnp.cos(li * step)
    ab_im = mag * jnp.sin(li * step)
    denom = lr * lr + li * li
    nr = ab_re - 1.0
    ni = ab_im
    coef_re = (nr * lr + ni * li) / denom
    coef_im = (ni * lr - nr * li) / denom
    br = br_ref[...]
    bi = bi_ref[...]
    bbr_ref[...] = coef_re * br - coef_im * bi
    bbi_ref[...] = coef_re * bi + coef_im * br
    lrf = jnp.minimum(lrf_ref[...], S5_EIG_CLIP)
    stepf = jnp.exp(stf_ref[...])
    n = pre_ref.shape[0]
    kk = (lax.broadcasted_iota(jnp.int32, (n, 1), 0) + 1).astype(F32)
    magp = jnp.exp(kk * (lrf * stepf))
    ang = kk * (lif_ref[...] * stepf)
    pre_ref[...] = magp * jnp.cos(ang)
    pim_ref[...] = magp * jnp.sin(ang)


def _s5_scan_kernel(u_ref, pm_ref, pmt_ref, bre_ref, bim_ref, cre_ref, cim_ref, d_ref,
                    pre_ref, pim_ref, o_ref, car_re, car_im, s_re, s_im, sb_re, sb_im, *, n):
    c = pl.program_id(1)
    nb = u_ref.shape[0]
    tps = bre_ref.shape[0]
    w = bre_ref.shape[2]
    chains = [(tt, b) for tt in range(tps) for b in range(nb)]
    ids = range(len(chains))
    lanes = lambda tt: slice(tt * LANES, (tt + 1) * LANES)
    states = lambda tt: slice(tt * w, (tt + 1) * w)

    @pl.when(c == 0)
    def _():
        car_re[...] = jnp.zeros_like(car_re)
        car_im[...] = jnp.zeros_like(car_im)

    sl = lambda i: pl.ds(i * SUBLANES, SUBLANES)
    sl2 = lambda i: pl.ds(i * 2 * SUBLANES, 2 * SUBLANES)
    up = [jnp.dot(pm_ref[...], u_ref[b, :, lanes(tt)], preferred_element_type=F32)
          for tt, b in chains]
    for k, (tt, b) in enumerate(chains):
        upb = up[k].astype(BF16)
        s_re[k] = jnp.dot(upb, bre_ref[tt], preferred_element_type=F32)
        s_im[k] = jnp.dot(upb, bim_ref[tt], preferred_element_type=F32)
    a_re = [jnp.broadcast_to(pre_ref[0:1, states(tt)], (SUBLANES, w)) for tt in range(tps)]
    a_im = [jnp.broadcast_to(pim_ref[0:1, states(tt)], (SUBLANES, w)) for tt in range(tps)]
    x_re = [jnp.zeros((SUBLANES, w), F32) for _ in ids]
    x_im = [jnp.zeros((SUBLANES, w), F32) for _ in ids]
    for i in range(n):
        for k, (tt, b) in enumerate(chains):
            n_re = a_re[tt] * x_re[k] - a_im[tt] * x_im[k] + s_re[k, sl(i), :]
            n_im = a_re[tt] * x_im[k] + a_im[tt] * x_re[k] + s_im[k, sl(i), :]
            x_re[k], x_im[k] = n_re, n_im
            s_re[k, sl(i), :] = n_re
            s_im[k, sl(i), :] = n_im
    cm_re, cm_im = [], []
    for k, (tt, b) in enumerate(chains):
        an_re = pre_ref[n - 1:n, states(tt)]
        an_im = pim_ref[n - 1:n, states(tt)]
        c_re = car_re[k]
        c_im = car_im[k]
        rows_re, rows_im = [c_re], [c_im]
        for s in range(1, SUBLANES + 1):
            e_re = x_re[k][s - 1:s, :]
            e_im = x_im[k][s - 1:s, :]
            c_re, c_im = (e_re + an_re * c_re - an_im * c_im,
                          e_im + an_re * c_im + an_im * c_re)
            if s < SUBLANES:
                rows_re.append(c_re)
                rows_im.append(c_im)
        car_re[k] = c_re
        car_im[k] = c_im
        cm_re.append(jnp.concatenate(rows_re + rows_re, axis=0))
        cm_im.append(jnp.concatenate(rows_im + rows_im, axis=0))
    for i in range(n // 2):
        p_re = [jnp.concatenate([jnp.broadcast_to(pre_ref[2 * i + r:2 * i + r + 1, states(tt)], (SUBLANES, w))
                                 for r in range(2)], axis=0) for tt in range(tps)]
        p_im = [jnp.concatenate([jnp.broadcast_to(pim_ref[2 * i + r:2 * i + r + 1, states(tt)], (SUBLANES, w))
                                 for r in range(2)], axis=0) for tt in range(tps)]
        for k, (tt, b) in enumerate(chains):
            sb_re[k, sl2(i), :] = (s_re[k, sl2(i), :] + p_re[tt] * cm_re[k] - p_im[tt] * cm_im[k]).astype(BF16)
            sb_im[k, sl2(i), :] = (s_im[k, sl2(i), :] + p_re[tt] * cm_im[k] + p_im[tt] * cm_re[k]).astype(BF16)
    for k, (tt, b) in enumerate(chains):
        yp = (jnp.dot(sb_re[k], cre_ref[tt], preferred_element_type=F32)
              - jnp.dot(sb_im[k], cim_ref[tt], preferred_element_type=F32)
              + d_ref[tt] * up[k])
        y = jnp.dot(pmt_ref[...], yp.astype(BF16), preferred_element_type=F32)
        cdf = 0.5 * (1.0 + jnp.tanh(np.float32(math.sqrt(2.0 / math.pi)) * (y + 0.044715 * (y * y * y))))
        o_ref[b, :, lanes(tt)] = (y * cdf).astype(o_ref.dtype)


def _s5_perm(tc):
    n = tc // SUBLANES
    pm = np.zeros((tc, tc), np.float32)
    for i in range(n):
        for s in range(SUBLANES):
            pm[i * SUBLANES + s, s * n + i] = 1.0
    return pm


def s5_scan(ug, lam_re, lam_im, log_step, b_re, b_im, c_re, c_im, d, bsz, seqlen, tc=256):
    g, p = lam_re.shape
    hh = b_re.shape[2]
    width = g * hh
    gpt = LANES // hh
    nt = width // LANES
    sw = gpt * p
    tc = min(tc, seqlen)
    n = tc // SUBLANES
    nc = seqlen // tc
    t = bsz * seqlen

    rep = lambda a: jnp.repeat(a.astype(F32), hh, axis=1)
    lr_rep, li_rep = rep(lam_re), rep(lam_im)
    st_rep = jnp.broadcast_to(log_step.astype(F32)[:, None], (g, p * hh))
    flat = lambda a: a.astype(F32).reshape(1, g * p)
    st_flat = jnp.broadcast_to(log_step.astype(F32)[:, None], (g, p)).reshape(1, g * p)
    bb_re, bb_im, ptab_re, ptab_im = pl.pallas_call(
        _s5_prep_kernel,
        out_shape=[jax.ShapeDtypeStruct((g, p * hh), F32)] * 2
        + [jax.ShapeDtypeStruct((n, g * p), F32)] * 2,
        name="s5_prep",
    )(lr_rep, li_rep, st_rep, b_re.astype(F32).reshape(g, p * hh), b_im.astype(F32).reshape(g, p * hh),
      flat(lam_re), flat(lam_im), st_flat)

    eye = jnp.eye(gpt, dtype=F32)
    bd_b = lambda bb: jnp.einsum('jgph,gk->jghkp', bb.reshape(nt, gpt, p, hh), eye
                                 ).reshape(nt, LANES, sw).astype(BF16)
    bd_c = lambda cc: jnp.einsum('jghp,gk->jkpgh', cc.astype(F32).reshape(nt, gpt, hh, p), eye
                                 ).reshape(nt, sw, LANES).astype(BF16)
    pm = _s5_perm(tc)
    tps = 2 if nt % 2 == 0 else 1
    nch = tps * bsz
    tile3 = lambda r, c: pl.BlockSpec((tps, r, c), lambda j, k: (j, 0, 0))
    full = lambda r, c: pl.BlockSpec((r, c), lambda j, k: (0, 0))
    out = pl.pallas_call(
        functools.partial(_s5_scan_kernel, n=n),
        out_shape=jax.ShapeDtypeStruct((bsz, seqlen, width), BF16),
        grid=(nt // tps, nc),
        in_specs=[pl.BlockSpec((bsz, tc, tps * LANES), lambda j, k: (0, k, j)),
                  full(tc, tc), full(tc, tc),
                  tile3(LANES, sw), tile3(LANES, sw), tile3(sw, LANES), tile3(sw, LANES),
                  tile3(1, LANES),
                  pl.BlockSpec((n, tps * sw), lambda j, k: (0, j)),
                  pl.BlockSpec((n, tps * sw), lambda j, k: (0, j))],
        out_specs=pl.BlockSpec((bsz, tc, tps * LANES), lambda j, k: (0, k, j)),
        scratch_shapes=[pltpu.VMEM((nch, 1, sw), F32), pltpu.VMEM((nch, 1, sw), F32),
                        pltpu.VMEM((nch, tc, sw), F32), pltpu.VMEM((nch, tc, sw), F32),
                        pltpu.VMEM((nch, tc, sw), BF16), pltpu.VMEM((nch, tc, sw), BF16)],
        compiler_params=_cparams(("parallel", "arbitrary")),
        name="s5_scan",
    )(ug.reshape(bsz, seqlen, ug.shape[1]), jnp.asarray(pm, BF16), jnp.asarray(pm.T, BF16),
      bd_b(bb_re), bd_b(bb_im), bd_c(c_re), bd_c(c_im), d.astype(F32).reshape(nt, 1, LANES),
      ptab_re, ptab_im)
    return out.reshape(t, width)


def _glu_kernel(a_ref, w_ref, b_ref, gn_ref, gate_ref, mixed_ref, o_ref):
    del mixed_ref
    z = jnp.dot(a_ref[...], w_ref[...], preferred_element_type=F32) + b_ref[...]
    gn = gn_ref[...].astype(F32)
    o_ref[...] = ((gn * _sigmoid(z)) * _silu(gate_ref[...].astype(F32))).astype(o_ref.dtype)


def s5_glu(gact, w_glu, b_glu, ug, mixed, col0, tm=512, tn=512):
    t, width = gact.shape
    tm, tn = _pick(t, tm), _pick(width, tn)
    while col0 % tn:
        tn -= LANES
    nj = width // tn
    return pl.pallas_call(
        _glu_kernel,
        out_shape=jax.ShapeDtypeStruct(mixed.shape, mixed.dtype),
        grid=(t // tm, nj),
        in_specs=[pl.BlockSpec((tm, width), lambda i, j: (i, 0)),
                  pl.BlockSpec((width, tn), lambda i, j: (0, j)),
                  pl.BlockSpec((1, tn), lambda i, j: (0, j)),
                  pl.BlockSpec((tm, tn), lambda i, j: (i, j)),
                  pl.BlockSpec((tm, tn), lambda i, j: (i, nj + j)),
                  pl.BlockSpec(memory_space=pl.ANY)],
        out_specs=pl.BlockSpec((tm, tn), lambda i, j: (i, col0 // tn + j)),
        input_output_aliases={5: 0},
        compiler_params=_cparams(("parallel", "arbitrary")),
        name="s5_glu",
    )(gact, w_glu.astype(BF16), b_glu.astype(F32).reshape(1, width), gact, ug, mixed)


def _conv_kernel(cur_ref, tail_ref, w_ref, b_ref, o_ref, *, halo):
    c = pl.program_id(1)
    cur = cur_ref[...].astype(F32)
    tail = jnp.where(c > 0, tail_ref[...].astype(F32), 0.0)
    ext = jnp.concatenate([tail, cur], axis=0)
    kw = w_ref.shape[0]
    acc = b_ref[...] + w_ref[kw - 1:kw, :] * cur
    for k in range(kw - 1):
        acc = acc + w_ref[k:k + 1, :] * pltpu.roll(ext, kw - 1 - k, axis=0)[halo:, :]
    o_ref[...] = _silu(acc).astype(o_ref.dtype)


def ssd_conv(xbc, conv_w, conv_b, bsz, seqlen, tc=512, wc=1024):
    t, ch = xbc.shape
    tc, wc = _pick(seqlen, tc), _pick(ch, wc)
    nc = seqlen // tc
    halo = 16
    hb = tc // halo
    return pl.pallas_call(
        functools.partial(_conv_kernel, halo=halo),
        out_shape=jax.ShapeDtypeStruct((t, ch), BF16),
        grid=(bsz, nc, ch // wc),
        in_specs=[pl.BlockSpec((tc, wc), lambda b, c, w: (b * nc + c, w)),
                  pl.BlockSpec((halo, wc), lambda b, c, w: (jnp.maximum((b * nc + c) * hb - 1, 0), w)),
                  pl.BlockSpec((conv_w.shape[0], wc), lambda b, c, w: (0, w)),
                  pl.BlockSpec((1, wc), lambda b, c, w: (0, w))],
        out_specs=pl.BlockSpec((tc, wc), lambda b, c, w: (b * nc + c, w)),
        compiler_params=_cparams(("parallel", "parallel", "parallel")),
        name="ssd_conv",
    )(xbc, xbc, conv_w.astype(F32), conv_b.astype(F32).reshape(1, ch))


def _ssd_decay_kernel(dt_ref, dtb_ref, alog_ref, tri_ref, dt_o, dtend_o, dout_o, cum_o):
    q = dt_ref.shape[0]
    dt = _softplus(dt_ref[...] + dtb_ref[...])
    la = dt * (-jnp.exp(alog_ref[...]))
    cum = jnp.dot(tri_ref[...], la, preferred_element_type=F32, precision=lax.Precision.HIGHEST)
    dt_o[...] = dt
    dtend_o[...] = dt * jnp.exp(cum[q - 1:q, :] - cum)
    dout_o[...] = jnp.exp(cum)
    cum_o[...] = cum


def _ssd_kernel(x_ref, b_ref, c_ref, z_ref, dt_ref, dtend_ref, dout_ref, cum_ref, e_ref, sel_ref,
                d_ref, nw_ref, o_ref, st_ref, *, hpg, hd):
    ck = pl.program_id(1)

    @pl.when(ck == 0)
    def _():
        st_ref[...] = jnp.zeros_like(st_ref)

    nb, q = x_ref.shape[0], x_ref.shape[1]
    bs = range(nb)
    e2 = e_ref[...]
    causal = (lax.broadcasted_iota(jnp.int32, (q, q), 0) >= lax.broadcasted_iota(jnp.int32, (q, q), 1))
    lane = lax.broadcasted_iota(jnp.int32, (q, LANES), 1)
    hpt = LANES // hd

    ex = []
    for b in bs:
        dout = dout_ref[b]
        v = jnp.concatenate([dt_ref[b], dtend_ref[b], dout,
                             jnp.broadcast_to(dout[q - 1:q, :], (SUBLANES, LANES))], axis=0)
        hi = v.astype(BF16)
        lo = (v - hi.astype(F32)).astype(BF16)
        ex.append(jnp.dot(jnp.concatenate([hi, lo], axis=1), e2, preferred_element_type=F32))
    x = [x_ref[b].astype(F32) for b in bs]
    xdt_b = [(x[b] * ex[b][:q]).astype(BF16) for b in bs]
    xde = [(x[b] * ex[b][q:2 * q]).astype(BF16) for b in bs]
    scores = [lax.dot_general(c_ref[b], b_ref[b], (((1,), (1,)), ((), ())), preferred_element_type=F32)
              for b in bs]
    cum_g = [jnp.dot(cum_ref[b], sel_ref[...], preferred_element_type=F32,
                     precision=lax.Precision.HIGHEST) for b in bs]
    cum_t = [cum_g[b].T for b in bs]
    parts = [[] for _ in bs]
    for tl in range(hpg // hpt):
        acc = [None for _ in bs]
        for hh in range(hpt):
            r = tl * hpt + hh
            for b in bs:
                seg = cum_g[b][:, r:r + 1] - cum_t[b][r:r + 1, :]
                wgt = (jnp.where(causal, jnp.exp(seg), 0.0) * scores[b]).astype(BF16)
                xt = xdt_b[b][:, tl * LANES:(tl + 1) * LANES]
                xm = jnp.where((lane >= hh * hd) & (lane < (hh + 1) * hd), xt, jnp.zeros_like(xt))
                part = jnp.dot(wgt, xm, preferred_element_type=F32)
                acc[b] = part if acc[b] is None else acc[b] + part
        for b in bs:
            parts[b].append(acc[b])
    for b in bs:
        st = st_ref[b]
        y_off = jnp.dot(c_ref[b], st.astype(BF16), preferred_element_type=F32) * ex[b][2 * q:3 * q]
        upd = lax.dot_general(b_ref[b], xde[b], (((0,), (0,)), ((), ())), preferred_element_type=F32)
        st_ref[b] = st * ex[b][3 * q:3 * q + 1] + upd
        y = jnp.concatenate(parts[b], axis=1) + y_off + d_ref[...] * x[b]
        y = y * _silu(z_ref[b].astype(F32))
        ms = jnp.mean(y * y, axis=-1, keepdims=True)
        o_ref[b] = ((y * lax.rsqrt(ms + NORM_EPS)) * nw_ref[...]).astype(o_ref.dtype)


def ssd_mix(xa, z, dt_raw, dt_bias, a_log, d, norm_w, mix_width, bsz, seqlen, q=128):
    t, w = z.shape
    nh = dt_bias.shape[0]
    hd = w // nh
    ng = SSD_GROUPS
    hpg = nh // ng
    gw = hpg * hd
    ns = (xa.shape[1] - w) // (2 * ng)
    assert ns == LANES and nh <= LANES and LANES % hd == 0 and hpg % (LANES // hd) == 0
    q = min(q, seqlen)
    nc = seqlen // q
    pad = lambda a: jnp.zeros((1, LANES), F32).at[0, :nh].set(a.astype(F32))
    e_np = np.zeros((ng, 2 * LANES, gw), np.float32)
    sel_np = np.zeros((ng, LANES, LANES), np.float32)
    for g in range(ng):
        for r in range(hpg):
            e_np[g, g * hpg + r, r * hd:(r + 1) * hd] = 1.0
            e_np[g, LANES + g * hpg + r, r * hd:(r + 1) * hd] = 1.0
            sel_np[g, g * hpg + r, r] = 1.0
    tri = np.tril(np.ones((q, q), np.float32))
    rowblk = pl.BlockSpec((q, LANES), lambda i: (i, 0))
    one = pl.BlockSpec((1, LANES), lambda i: (0, 0))
    dec = pl.pallas_call(
        _ssd_decay_kernel,
        out_shape=[jax.ShapeDtypeStruct((t, LANES), F32)] * 4,
        grid=(t // q,),
        in_specs=[rowblk, one, one, pl.BlockSpec((q, q), lambda i: (0, 0))],
        out_specs=[rowblk] * 4,
        compiler_params=_cparams(("parallel",)),
        name="ssd_decay",
    )(dt_raw, pad(dt_bias), pad(a_log), jnp.asarray(tri))
    dec = [a.reshape(bsz, seqlen, LANES) for a in dec]
    xa3 = xa.reshape(bsz, seqlen, xa.shape[1])
    blk = lambda wd, off: pl.BlockSpec((bsz, q, wd), lambda g, c: (0, c, off + g))
    hblk = pl.BlockSpec((bsz, q, LANES), lambda g, c: (0, c, 0))
    out = pl.pallas_call(
        functools.partial(_ssd_kernel, hpg=hpg, hd=hd),
        out_shape=jax.ShapeDtypeStruct((bsz, seqlen, mix_width), BF16),
        grid=(ng, nc),
        in_specs=[blk(gw, 0), blk(ns, w // ns), blk(ns, w // ns + ng), blk(gw, 0),
                  hblk, hblk, hblk, hblk,
                  pl.BlockSpec((None, 2 * LANES, gw), lambda g, c: (g, 0, 0)),
                  pl.BlockSpec((None, LANES, LANES), lambda g, c: (g, 0, 0)),
                  pl.BlockSpec((1, gw), lambda g, c: (0, g)),
                  pl.BlockSpec((1, gw), lambda g, c: (0, g))],
        out_specs=blk(gw, 0),
        scratch_shapes=[pltpu.VMEM((bsz, ns, gw), F32)],
        compiler_params=_cparams(("parallel", "arbitrary")),
        name="ssd_mix",
    )(xa3, xa3, xa3, z.reshape(bsz, seqlen, w), *dec, jnp.asarray(e_np, BF16), jnp.asarray(sel_np),
      jnp.repeat(d.astype(F32), hd).reshape(1, w), norm_w.astype(F32).reshape(1, w))
    return out.reshape(t, mix_width)


def _logf_cumsum_kernel(f_ref, b_ref, tri_ref, o_ref, car_ref):
    c = pl.program_id(1)

    @pl.when(c == 0)
    def _():
        car_ref[...] = jnp.zeros_like(car_ref)

    lf = -_softplus(-(f_ref[...] + b_ref[...]))
    cum = jnp.dot(tri_ref[...], lf, preferred_element_type=F32,
                  precision=lax.Precision.HIGHEST) + car_ref[...]
    o_ref[...] = cum
    car_ref[...] = cum[cum.shape[0] - 1:, :]


def logf_cumsum(f_raw, b_f, bsz, seqlen, tc=256):
    t = f_raw.shape[0]
    nh = b_f.shape[0]
    tc = min(tc, seqlen)
    nc = seqlen // tc
    b_pad = jnp.zeros((1, LANES), F32).at[0, :nh].set(b_f.astype(F32))
    tri = np.tril(np.ones((tc, tc), np.float32))
    return pl.pallas_call(
        _logf_cumsum_kernel,
        out_shape=jax.ShapeDtypeStruct((t, LANES), F32),
        grid=(bsz, nc),
        in_specs=[pl.BlockSpec((tc, LANES), lambda b, c: (b * nc + c, 0)),
                  pl.BlockSpec((1, LANES), lambda b, c: (0, 0)),
                  pl.BlockSpec((tc, tc), lambda b, c: (0, 0))],
        out_specs=pl.BlockSpec((tc, LANES), lambda b, c: (b * nc + c, 0)),
        scratch_shapes=[pltpu.VMEM((1, LANES), F32)],
        compiler_params=_cparams(("parallel", "arbitrary")),
        name="logf_cumsum",
    )(f_raw, b_pad, jnp.asarray(tri))


def _fox_kernel(q_ref, k_ref, v_ref, cum_ref, gate_ref, o_ref, qa_ref, ka_ref, va_ref, s_ref, p_ref,
                m_ref, al_ref, acc_ref, *, scale, tb):
    h = pl.program_id(1)
    qi = pl.program_id(2)
    hd = q_ref.shape[1]
    reps = tb // LANES

    @pl.when(qi == 0)
    def _():
        nk = k_ref.shape[0]
        onehot = (lax.broadcasted_iota(jnp.int32, (LANES, LANES), 0) == h).astype(F32)
        c2 = jnp.dot(cum_ref[...], onehot, preferred_element_type=F32,
                     precision=lax.Precision.HIGHEST) * (-LOG2E)
        hi = c2.astype(BF16)
        r1 = c2 - hi.astype(F32)
        mid = r1.astype(BF16)
        lo = (r1 - mid.astype(F32)).astype(BF16)
        lane = lax.broadcasted_iota(jnp.int32, (nk, LANES), 1)
        zero = jnp.zeros((nk, LANES), BF16)
        ka_ref[:, :hd] = k_ref[...]
        ka_ref[:, hd:] = jnp.where(lane == 0, hi, jnp.where(lane == 1, mid, jnp.where(lane == 2, lo, zero)))
        va_ref[:, :hd] = v_ref[...]
        va_ref[:, hd:] = jnp.ones((nk, LANES), BF16)

    qa_ref[:, :hd] = (q_ref[...].astype(F32) * (scale * LOG2E)).astype(BF16)
    qa_ref[:, hd:] = jnp.where(lax.broadcasted_iota(jnp.int32, (tb, LANES), 1) < 3, 1.0, 0.0).astype(BF16)

    def rows(j):
        return slice(j * tb, (j + 1) * tb)

    def qk(j):
        return lax.dot_general(qa_ref[...], ka_ref[rows(j), :], (((1,), (1,)), ((), ())),
                               preferred_element_type=F32)

    def softmax_stage(j, masked):
        t = s_ref[j % 2]
        if masked:
            keep = (lax.broadcasted_iota(jnp.int32, (tb, tb), 0) >= lax.broadcasted_iota(jnp.int32, (tb, tb), 1))
            t = jnp.where(keep, t, -jnp.inf)
        m = m_ref[...]
        m_new = jnp.maximum(m, jnp.max(t, axis=1, keepdims=True))
        al_ref[...] = jnp.exp2(m - m_new)
        m_ref[...] = m_new
        p_ref[j % 2] = jnp.exp2(t - jnp.concatenate([m_new] * reps, axis=1)).astype(BF16)

    def pv_stage(j):
        alpha = al_ref[...]
        acc_ref[...] = (jnp.concatenate([alpha, alpha], axis=1) * acc_ref[...]
                        + jnp.dot(p_ref[j % 2], va_ref[rows(j), :], preferred_element_type=F32))

    for nkb in range(1, ka_ref.shape[0] // tb + 1):
        @pl.when(qi == nkb - 1)
        def _(nkb=nkb):
            m_ref[...] = jnp.full_like(m_ref, -jnp.inf)
            acc_ref[...] = jnp.zeros_like(acc_ref)
            s_ref[0] = qk(0)
            for j in range(nkb):
                if j > 0:
                    pv_stage(j - 1)
                softmax_stage(j, j == nkb - 1)
                if j + 1 < nkb:
                    s_ref[(j + 1) % 2] = qk(j + 1)
            pv_stage(nkb - 1)
            acc = acc_ref[...]
            att = acc[:, :hd] / acc[:, hd:]
            o_ref[...] = (att * _silu(gate_ref[...].astype(F32))).astype(o_ref.dtype)


def fox_attention(qkvg, cum, nh, hd, bsz, seqlen, tb=1024):
    t = qkvg.shape[0]
    tb = min(tb, seqlen)
    nb = seqlen // tb
    assert hd == LANES
    return pl.pallas_call(
        functools.partial(_fox_kernel, scale=1.0 / math.sqrt(hd), tb=tb),
        out_shape=jax.ShapeDtypeStruct((t, nh * hd), BF16),
        grid=(bsz, nh, nb),
        in_specs=[pl.BlockSpec((tb, hd), lambda b, h, i: (b * nb + i, h)),
                  pl.BlockSpec((seqlen, hd), lambda b, h, i: (b, nh + h)),
                  pl.BlockSpec((seqlen, hd), lambda b, h, i: (b, 2 * nh + h)),
                  pl.BlockSpec((seqlen, LANES), lambda b, h, i: (b, 0)),
                  pl.BlockSpec((tb, hd), lambda b, h, i: (b * nb + i, 3 * nh + h))],
        out_specs=pl.BlockSpec((tb, hd), lambda b, h, i: (b * nb + i, h)),
        scratch_shapes=[pltpu.VMEM((tb, 2 * hd), BF16), pltpu.VMEM((seqlen, 2 * hd), BF16),
                        pltpu.VMEM((seqlen, 2 * hd), BF16), pltpu.VMEM((2, tb, tb), F32),
                        pltpu.VMEM((2, tb, tb), BF16), pltpu.VMEM((tb, LANES), F32),
                        pltpu.VMEM((tb, LANES), F32), pltpu.VMEM((tb, 2 * hd), F32)],
        compiler_params=_cparams(("parallel", "parallel", "arbitrary")),
        name="fox_attention",
    )(qkvg, qkvg, qkvg, cum, qkvg)


def _pad_cols(w, n):
    return jnp.pad(w, ((0, 0), (0, n - w.shape[1])))


def kernel(x, l0_norm_w, l0_w_in, l0_s5_lambda_re, l0_s5_lambda_im, l0_s5_log_step, l0_s5_b_re, l0_s5_b_im, l0_s5_c_re, l0_s5_c_im, l0_s5_d, l0_s5_w_glu, l0_s5_b_glu, l0_ssd_conv_w, l0_ssd_conv_b, l0_ssd_dt_bias, l0_ssd_a_log, l0_ssd_d, l0_ssd_norm_w, l0_w_out, l1_norm_w, l1_w_in, l1_fox_b_f, l1_w_out, final_norm_w):
    bsz, seqlen, dm = x.shape
    t = bsz * seqlen
    x2 = x.reshape(t, dm)

    s5_w = l0_s5_w_glu.shape[0]
    ssd_w = l0_ssd_norm_w.shape[0]
    xbc_w = l0_ssd_conv_w.shape[1]
    ssd_h = l0_ssd_dt_bias.shape[0]
    mix_w = s5_w + ssd_w
    o_z = 2 * s5_w
    o_xbc = o_z + ssd_w
    o_dt = o_xbc + xbc_w

    h0 = rmsnorm(x2, l0_norm_w, BF16)
    w0 = l0_w_in.astype(BF16)
    ug = matmul(h0, w0, BF16, n=o_z)
    z = matmul(h0, w0, BF16, n=ssd_w, col0=o_z)
    xbc = matmul(h0, w0, BF16, n=xbc_w, col0=o_xbc)
    dt_raw = matmul(h0, _pad_cols(w0[:, o_dt:o_dt + ssd_h], LANES), F32)

    gact = s5_scan(ug, l0_s5_lambda_re, l0_s5_lambda_im, l0_s5_log_step, l0_s5_b_re, l0_s5_b_im,
                   l0_s5_c_re, l0_s5_c_im, l0_s5_d, bsz, seqlen)
    xa = ssd_conv(xbc, l0_ssd_conv_w, l0_ssd_conv_b, bsz, seqlen)
    mixed = ssd_mix(xa, z, dt_raw, l0_ssd_dt_bias, l0_ssd_a_log, l0_ssd_d, l0_ssd_norm_w,
                    mix_w, bsz, seqlen)
    mixed = s5_glu(gact, l0_s5_w_glu, l0_s5_b_glu, ug, mixed, ssd_w)
    x1 = matmul(mixed, l0_w_out.astype(BF16), F32, res=x2, rot=s5_w, tm=512, tn=512)

    fox_w = l1_w_out.shape[0]
    nh = l1_fox_b_f.shape[0]
    hd = fox_w // nh
    h1 = rmsnorm(x1, l1_norm_w, BF16)
    w1 = l1_w_in.astype(BF16)
    qkvg = matmul(h1, w1, BF16, n=4 * fox_w)
    f_raw = matmul(h1, _pad_cols(w1[:, 4 * fox_w:4 * fox_w + nh], LANES), F32)
    cum = logf_cumsum(f_raw, l1_fox_b_f, bsz, seqlen)
    att = fox_attention(qkvg, cum, nh, hd, bsz, seqlen)
    x2_out = matmul(att, l1_w_out.astype(BF16), F32, res=x1, tn=512)

    return rmsnorm(x2_out, final_norm_w, F32).reshape(bsz, seqlen, dm)
```

```python
import functools
import math

import numpy as np
import jax
import jax.numpy as jnp
from jax import lax
from jax.experimental import pallas as pl
from jax.experimental.pallas import tpu as pltpu

F32 = jnp.float32
BF16 = jnp.bfloat16

NORM_EPS = 1e-5
S5_EIG_CLIP = -1e-4
SSD_GROUPS = 8
LANES = 128
SUBLANES = 8
VMEM_LIMIT = 56 * 1024 * 1024
LOG2E = 1.4426950408889634


def _cparams(sem):
    return pltpu.CompilerParams(dimension_semantics=sem, vmem_limit_bytes=VMEM_LIMIT)


def _pick(n, pref):
    if n <= pref:
        return n
    t = pref
    while t >= LANES:
        if n % t == 0:
            return t
        t -= LANES
    return n


def _sigmoid(x):
    return 1.0 / (1.0 + jnp.exp(-x))


def _silu(x):
    return x * _sigmoid(x)


def _softplus(x):
    return jnp.maximum(x, 0.0) + jnp.log1p(jnp.exp(-jnp.abs(x)))


def _rmsnorm_kernel(x_ref, w_ref, o_ref):
    x = x_ref[...]
    ms = jnp.mean(x * x, axis=-1, keepdims=True)
    o_ref[...] = ((x * lax.rsqrt(ms + NORM_EPS)) * w_ref[...]).astype(o_ref.dtype)


def rmsnorm(x2d, w, out_dtype):
    t, d = x2d.shape
    tm = _pick(t, 256)
    return pl.pallas_call(
        _rmsnorm_kernel,
        out_shape=jax.ShapeDtypeStruct((t, d), out_dtype),
        grid=(t // tm,),
        in_specs=[pl.BlockSpec((tm, d), lambda i: (i, 0)),
                  pl.BlockSpec((1, d), lambda i: (0, 0))],
        out_specs=pl.BlockSpec((tm, d), lambda i: (i, 0)),
        compiler_params=_cparams(("parallel",)),
        name="rmsnorm",
    )(x2d, w.reshape(1, d).astype(F32))


def _mm_kernel(a_ref, b_ref, *rest, has_res, rot):
    if has_res:
        r_ref, o_ref = rest
    else:
        (o_ref,) = rest
    kd = a_ref.shape[1]
    if rot:
        acc = (jnp.dot(a_ref[:, :kd - rot], b_ref[rot:, :], preferred_element_type=F32)
               + jnp.dot(a_ref[:, kd - rot:], b_ref[:rot, :], preferred_element_type=F32))
    else:
        acc = jnp.dot(a_ref[...], b_ref[...], preferred_element_type=F32)
    if has_res:
        acc = r_ref[...] + acc
    o_ref[...] = acc.astype(o_ref.dtype)


def matmul(a, b, out_dtype, n=None, col0=0, res=None, rot=0, tm=1024, tn=1024):
    m, kd = a.shape
    n = b.shape[1] if n is None else n
    tm, tn = _pick(m, tm), _pick(n, tn)
    while col0 % tn or n % tn:
        tn -= LANES
    cb = col0 // tn
    in_specs = [pl.BlockSpec((tm, kd), lambda i, j: (i, 0)),
                pl.BlockSpec((kd, tn), lambda i, j: (0, cb + j))]
    args = [a, b]
    if res is not None:
        in_specs.append(pl.BlockSpec((tm, tn), lambda i, j: (i, j)))
        args.append(res)
    return pl.pallas_call(
        functools.partial(_mm_kernel, has_res=res is not None, rot=rot),
        out_shape=jax.ShapeDtypeStruct((m, n), out_dtype),
        grid=(m // tm, n // tn),
        in_specs=in_specs,
        out_specs=pl.BlockSpec((tm, tn), lambda i, j: (i, j)),
        compiler_params=_cparams(("parallel", "parallel")),
        name="matmul",
    )(*args)


def _s5_prep_kernel(lr_ref, li_ref, st_ref, br_ref, bi_ref, lrf_ref, lif_ref, stf_ref,
                    bbr_ref, bbi_ref, pre_ref, pim_ref):
    lr = jnp.minimum(lr_ref[...], S5_EIG_CLIP)
    li = li_ref[...]
    step = jnp.exp(st_ref[...])
    mag = jnp.exp(lr * step)
    ab_re = mag * jnp.cos(li * step)
    ab_im = mag * jnp.sin(li * step)
    denom = lr * lr + li * li
    nr = ab_re - 1.0
    ni = ab_im
    coef_re = (nr * lr + ni * li) / denom
    coef_im = (ni * lr - nr * li) / denom
    br = br_ref[...]
    bi = bi_ref[...]
    bbr_ref[...] = coef_re * br - coef_im * bi
    bbi_ref[...] = coef_re * bi + coef_im * br
    lrf = jnp.minimum(lrf_ref[...], S5_EIG_CLIP)
    stepf = jnp.exp(stf_ref[...])
    n = pre_ref.shape[0]
    kk = (lax.broadcasted_iota(jnp.int32, (n, 1), 0) + 1).astype(F32)
    magp = jnp.exp(kk * (lrf * stepf))
    ang = kk * (lif_ref[...] * stepf)
    pre_ref[...] = magp * jnp.cos(ang)
    pim_ref[...] = magp * jnp.sin(ang)


def _s5_scan_kernel(u_ref, pm_ref, pmt_ref, bre_ref, bim_ref, cre_ref, cim_ref, d_ref,
                    pre_ref, pim_ref, o_ref, car_re, car_im, s_re, s_im, sb_re, sb_im, *, n):
    c = pl.program_id(1)
    nb = u_ref.shape[0]
    tps = bre_ref.shape[0]
    w = bre_ref.shape[2]
    chains = [(tt, b) for tt in range(tps) for b in range(nb)]
    ids = range(len(chains))
    lanes = lambda tt: slice(tt * LANES, (tt + 1) * LANES)
    states = lambda tt: slice(tt * w, (tt + 1) * w)

    @pl.when(c == 0)
    def _():
        car_re[...] = jnp.zeros_like(car_re)
        car_im[...] = jnp.zeros_like(car_im)

    sl = lambda i: pl.ds(i * SUBLANES, SUBLANES)
    sl2 = lambda i: pl.ds(i * 2 * SUBLANES, 2 * SUBLANES)
    up = [jnp.dot(pm_ref[...], u_ref[b, :, lanes(tt)], preferred_element_type=F32)
          for tt, b in chains]
    for k, (tt, b) in enumerate(chains):
        upb = up[k].astype(BF16)
        s_re[k] = jnp.dot(upb, bre_ref[tt], preferred_element_type=F32)
        s_im[k] = jnp.dot(upb, bim_ref[tt], preferred_element_type=F32)
    a_re = [jnp.broadcast_to(pre_ref[0:1, states(tt)], (SUBLANES, w)) for tt in range(tps)]
    a_im = [jnp.broadcast_to(pim_ref[0:1, states(tt)], (SUBLANES, w)) for tt in range(tps)]
    x_re = [jnp.zeros((SUBLANES, w), F32) for _ in ids]
    x_im = [jnp.zeros((SUBLANES, w), F32) for _ in ids]
    for i in range(n):
        for k, (tt, b) in enumerate(chains):
            n_re = a_re[tt] * x_re[k] - a_im[tt] * x_im[k] + s_re[k, sl(i), :]
            n_im = a_re[tt] * x_im[k] + a_im[tt] * x_re[k] + s_im[k, sl(i), :]
            x_re[k], x_im[k] = n_re, n_im
            s_re[k, sl(i), :] = n_re
            s_im[k, sl(i), :] = n_im
    cm_re, cm_im = [], []
    for k, (tt, b) in enumerate(chains):
        an_re = pre_ref[n - 1:n, states(tt)]
        an_im = pim_ref[n - 1:n, states(tt)]
        c_re = car_re[k]
        c_im = car_im[k]
        rows_re, rows_im = [c_re], [c_im]
        for s in range(1, SUBLANES + 1):
            e_re = x_re[k][s - 1:s, :]
            e_im = x_im[k][s - 1:s, :]
            c_re, c_im = (e_re + an_re * c_re - an_im * c_im,
                          e_im + an_re * c_im + an_im * c_re)
            if s < SUBLANES:
                rows_re.append(c_re)
                rows_im.append(c_im)
        car_re[k] = c_re
        car_im[k] = c_im
        cm_re.append(jnp.concatenate(rows_re + rows_re, axis=0))
        cm_im.append(jnp.concatenate(rows_im + rows_im, axis=0))
    for i in range(n // 2):
        p_re = [jnp.concatenate([jnp.broadcast_to(pre_ref[2 * i + r:2 * i + r + 1, states(tt)], (SUBLANES, w))
                                 for r in range(2)], axis=0) for tt in range(tps)]
        p_im = [jnp.concatenate([jnp.broadcast_to(pim_ref[2 * i + r:2 * i + r + 1, states(tt)], (SUBLANES, w))
                                 for r in range(2)], axis=0) for tt in range(tps)]
        for k, (tt, b) in enumerate(chains):
            sb_re[k, sl2(i), :] = (s_re[k, sl2(i), :] + p_re[tt] * cm_re[k] - p_im[tt] * cm_im[k]).astype(BF16)
            sb_im[k, sl2(i), :] = (s_im[k, sl2(i), :] + p_re[tt] * cm_im[k] + p_im[tt] * cm_re[k]).astype(BF16)
    for k, (tt, b) in enumerate(chains):
        yp = (jnp.dot(sb_re[k], cre_ref[tt], preferred_element_type=F32)
              - jnp.dot(sb_im[k], cim_ref[tt], preferred_element_type=F32)
              + d_ref[tt] * up[k])
        y = jnp.dot(pmt_ref[...], yp.astype(BF16), preferred_element_type=F32)
        cdf = 0.5 * (1.0 + jnp.tanh(np.float32(math.sqrt(2.0 / math.pi)) * (y + 0.044715 * (y * y * y))))
        o_ref[b, :, lanes(tt)] = (y * cdf).astype(o_ref.dtype)


def _s5_perm(tc):
    n = tc // SUBLANES
    pm = np.zeros((tc, tc), np.float32)
    for i in range(n):
        for s in range(SUBLANES):
            pm[i * SUBLANES + s, s * n + i] = 1.0
    return pm


def s5_scan(ug, lam_re, lam_im, log_step, b_re, b_im, c_re, c_im, d, bsz, seqlen, tc=256):
    g, p = lam_re.shape
    hh = b_re.shape[2]
    width = g * hh
    gpt = LANES // hh
    nt = width // LANES
    sw = gpt * p
    tc = min(tc, seqlen)
    n = tc // SUBLANES
    nc = seqlen // tc
    t = bsz * seqlen

    rep = lambda a: jnp.repeat(a.astype(F32), hh, axis=1)
    lr_rep, li_rep = rep(lam_re), rep(lam_im)
    st_rep = jnp.broadcast_to(log_step.astype(F32)[:, None], (g, p * hh))
    flat = lambda a: a.astype(F32).reshape(1, g * p)
    st_flat = jnp.broadcast_to(log_step.astype(F32)[:, None], (g, p)).reshape(1, g * p)
    bb_re, bb_im, ptab_re, ptab_im = pl.pallas_call(
        _s5_prep_kernel,
        out_shape=[jax.ShapeDtypeStruct((g, p * hh), F32)] * 2
        + [jax.ShapeDtypeStruct((n, g * p), F32)] * 2,
        name="s5_prep",
    )(lr_rep, li_rep, st_rep, b_re.astype(F32).reshape(g, p * hh), b_im.astype(F32).reshape(g, p * hh),
      flat(lam_re), flat(lam_im), st_flat)

    eye = jnp.eye(gpt, dtype=F32)
    bd_b = lambda bb: jnp.einsum('jgph,gk->jghkp', bb.reshape(nt, gpt, p, hh), eye
                                 ).reshape(nt, LANES, sw).astype(BF16)
    bd_c = lambda cc: jnp.einsum('jghp,gk->jkpgh', cc.astype(F32).reshape(nt, gpt, hh, p), eye
                                 ).reshape(nt, sw, LANES).astype(BF16)
    pm = _s5_perm(tc)
    tps = 2 if nt % 2 == 0 else 1
    nch = tps * bsz
    tile3 = lambda r, c: pl.BlockSpec((tps, r, c), lambda j, k: (j, 0, 0))
    full = lambda r, c: pl.BlockSpec((r, c), lambda j, k: (0, 0))
    out = pl.pallas_call(
        functools.partial(_s5_scan_kernel, n=n),
        out_shape=jax.ShapeDtypeStruct((bsz, seqlen, width), BF16),
        grid=(nt // tps, nc),
        in_specs=[pl.BlockSpec((bsz, tc, tps * LANES), lambda j, k: (0, k, j)),
                  full(tc, tc), full(tc, tc),
                  tile3(LANES, sw), tile3(LANES, sw), tile3(sw, LANES), tile3(sw, LANES),
                  tile3(1, LANES),
                  pl.BlockSpec((n, tps * sw), lambda j, k: (0, j)),
                  pl.BlockSpec((n, tps * sw), lambda j, k: (0, j))],
        out_specs=pl.BlockSpec((bsz, tc, tps * LANES), lambda j, k: (0, k, j)),
        scratch_shapes=[pltpu.VMEM((nch, 1, sw), F32), pltpu.VMEM((nch, 1, sw), F32),
                        pltpu.VMEM((nch, tc, sw), F32), pltpu.VMEM((nch, tc, sw), F32),
                        pltpu.VMEM((nch, tc, sw), BF16), pltpu.VMEM((nch, tc, sw), BF16)],
        compiler_params=_cparams(("parallel", "arbitrary")),
        name="s5_scan",
    )(ug.reshape(bsz, seqlen, ug.shape[1]), jnp.asarray(pm, BF16), jnp.asarray(pm.T, BF16),
      bd_b(bb_re), bd_b(bb_im), bd_c(c_re), bd_c(c_im), d.astype(F32).reshape(nt, 1, LANES),
      ptab_re, ptab_im)
    return out.reshape(t, width)


def _glu_kernel(a_ref, w_ref, b_ref, gn_ref, gate_ref, mixed_ref, o_ref):
    del mixed_ref
    z = jnp.dot(a_ref[...], w_ref[...], preferred_element_type=F32) + b_ref[...]
    gn = gn_ref[...].astype(F32)
    o_ref[...] = ((gn * _sigmoid(z)) * _silu(gate_ref[...].astype(F32))).astype(o_ref.dtype)


def s5_glu(gact, w_glu, b_glu, ug, mixed, col0, tm=512, tn=512):
    t, width = gact.shape
    tm, tn = _pick(t, tm), _pick(width, tn)
    while col0 % tn:
        tn -= LANES
    nj = width // tn
    return pl.pallas_call(
        _glu_kernel,
        out_shape=jax.ShapeDtypeStruct(mixed.shape, mixed.dtype),
        grid=(t // tm, nj),
        in_specs=[pl.BlockSpec((tm, width), lambda i, j: (i, 0)),
                  pl.BlockSpec((width, tn), lambda i, j: (0, j)),
                  pl.BlockSpec((1, tn), lambda i, j: (0, j)),
                  pl.BlockSpec((tm, tn), lambda i, j: (i, j)),
                  pl.BlockSpec((tm, tn), lambda i, j: (i, nj + j)),
                  pl.BlockSpec(memory_space=pl.ANY)],
        out_specs=pl.BlockSpec((tm, tn), lambda i, j: (i, col0 // tn + j)),
        input_output_aliases={5: 0},
        compiler_params=_cparams(("parallel", "arbitrary")),
        name="s5_glu",
    )(gact, w_glu.astype(BF16), b_glu.astype(F32).reshape(1, width), gact, ug, mixed)


def _conv_kernel(cur_ref, tail_ref, w_ref, b_ref, o_ref, *, halo):
    c = pl.program_id(1)
    cur = cur_ref[...].astype(F32)
    tail = jnp.where(c > 0, tail_ref[...].astype(F32), 0.0)
    ext = jnp.concatenate([tail, cur], axis=0)
    kw = w_ref.shape[0]
    acc = b_ref[...] + w_ref[kw - 1:kw, :] * cur
    for k in range(kw - 1):
        acc = acc + w_ref[k:k + 1, :] * pltpu.roll(ext, kw - 1 - k, axis=0)[halo:, :]
    o_ref[...] = _silu(acc).astype(o_ref.dtype)


def ssd_conv(xbc, conv_w, conv_b, bsz, seqlen, tc=512, wc=1024):
    t, ch = xbc.shape
    tc, wc = _pick(seqlen, tc), _pick(ch, wc)
    nc = seqlen // tc
    halo = 16
    hb = tc // halo
    return pl.pallas_call(
        functools.partial(_conv_kernel, halo=halo),
        out_shape=jax.ShapeDtypeStruct((t, ch), BF16),
        grid=(bsz, nc, ch // wc),
        in_specs=[pl.BlockSpec((tc, wc), lambda b, c, w: (b * nc + c, w)),
                  pl.BlockSpec((halo, wc), lambda b, c, w: (jnp.maximum((b * nc + c) * hb - 1, 0), w)),
                  pl.BlockSpec((conv_w.shape[0], wc), lambda b, c, w: (0, w)),
                  pl.BlockSpec((1, wc), lambda b, c, w: (0, w))],
        out_specs=pl.BlockSpec((tc, wc), lambda b, c, w: (b * nc + c, w)),
        compiler_params=_cparams(("parallel", "parallel", "parallel")),
        name="ssd_conv",
    )(xbc, xbc, conv_w.astype(F32), conv_b.astype(F32).reshape(1, ch))


def _ssd_decay_kernel(dt_ref, dtb_ref, alog_ref, tri_ref, sel_ref, selpad_ref,
                      dt_o, dtend_o, dout_o, cumg_o, cumt_o):
    q = dt_ref.shape[0]
    dt = _softplus(dt_ref[...] + dtb_ref[...])
    la = dt * (-jnp.exp(alog_ref[...]))
    cum = jnp.dot(tri_ref[...], la, preferred_element_type=F32, precision=lax.Precision.HIGHEST)
    dt_o[...] = dt
    dtend_o[...] = dt * jnp.exp(cum[q - 1:q, :] - cum)
    dout_o[...] = jnp.exp(cum)
    for g in range(sel_ref.shape[0]):
        cumg_o[:, g * LANES:(g + 1) * LANES] = jnp.dot(cum, sel_ref[g], preferred_element_type=F32,
                                                       precision=lax.Precision.HIGHEST)
    cumt_o[...] = jnp.dot(cum, selpad_ref[...], preferred_element_type=F32,
                          precision=lax.Precision.HIGHEST).T


def _ssd_kernel(x_ref, b_ref, c_ref, z_ref, dt_ref, dtend_ref, dout_ref, cumg_ref, cumt_ref, e_ref,
                d_ref, nw_ref, o_ref, st_ref, *, hpg, hd, hp):
    ck = pl.program_id(1)

    @pl.when(ck == 0)
    def _():
        st_ref[...] = jnp.zeros_like(st_ref)

    nb, q = x_ref.shape[0], x_ref.shape[1]
    gps = e_ref.shape[0]
    gw = e_ref.shape[2]
    ns = b_ref.shape[2] // gps
    chains = [(gg, b) for gg in range(gps) for b in range(nb)]
    ids = range(len(chains))
    chan = lambda gg: slice(gg * gw, (gg + 1) * gw)
    stat = lambda gg: slice(gg * ns, (gg + 1) * ns)
    causal = (lax.broadcasted_iota(jnp.int32, (q, q), 0) >= lax.broadcasted_iota(jnp.int32, (q, q), 1))
    lane = lax.broadcasted_iota(jnp.int32, (q, LANES), 1)
    hpt = LANES // hd

    vs = []
    for b in range(nb):
        dout = dout_ref[b]
        v = jnp.concatenate([dt_ref[b], dtend_ref[b], dout,
                             jnp.broadcast_to(dout[q - 1:q, :], (SUBLANES, LANES))], axis=0)
        hi = v.astype(BF16)
        vs.append(jnp.concatenate([hi, (v - hi.astype(F32)).astype(BF16)], axis=1))
    ex = [jnp.dot(vs[b], e_ref[gg], preferred_element_type=F32) for gg, b in chains]
    x = [x_ref[b, :, chan(gg)].astype(F32) for gg, b in chains]
    bm = [b_ref[b, :, stat(gg)] for gg, b in chains]
    cm = [c_ref[b, :, stat(gg)] for gg, b in chains]
    xdt_b = [(x[k] * ex[k][:q]).astype(BF16) for k in ids]
    xde = [(x[k] * ex[k][q:2 * q]).astype(BF16) for k in ids]
    scores = [lax.dot_general(cm[k], bm[k], (((1,), (1,)), ((), ())), preferred_element_type=F32)
              for k in ids]
    parts = [[] for _ in ids]
    for tl in range(hpg // hpt):
        acc = [None for _ in ids]
        for hh in range(hpt):
            r = tl * hpt + hh
            for k, (gg, b) in enumerate(chains):
                seg = (cumg_ref[b, :, gg * LANES + r:gg * LANES + r + 1]
                       - cumt_ref[b, gg * hp + r:gg * hp + r + 1, :])
                wgt = (jnp.where(causal, jnp.exp(seg), 0.0) * scores[k]).astype(BF16)
                xt = xdt_b[k][:, tl * LANES:(tl + 1) * LANES]
                xm = jnp.where((lane >= hh * hd) & (lane < (hh + 1) * hd), xt, jnp.zeros_like(xt))
                part = jnp.dot(wgt, xm, preferred_element_type=F32)
                acc[k] = part if acc[k] is None else acc[k] + part
        for k in ids:
            parts[k].append(acc[k])
    for k, (gg, b) in enumerate(chains):
        st = st_ref[k]
        y_off = jnp.dot(cm[k], st.astype(BF16), preferred_element_type=F32) * ex[k][2 * q:3 * q]
        upd = lax.dot_general(bm[k], xde[k], (((0,), (0,)), ((), ())), preferred_element_type=F32)
        st_ref[k] = st * ex[k][3 * q:3 * q + 1] + upd
        y = jnp.concatenate(parts[k], axis=1) + y_off + d_ref[:, chan(gg)] * x[k]
        y = y * _silu(z_ref[b, :, chan(gg)].astype(F32))
        ms = jnp.mean(y * y, axis=-1, keepdims=True)
        o_ref[b, :, chan(gg)] = ((y * lax.rsqrt(ms + NORM_EPS)) * nw_ref[:, chan(gg)]).astype(o_ref.dtype)


def ssd_mix(xa, z, dt_raw, dt_bias, a_log, d, norm_w, mix_width, bsz, seqlen, q=128):
    t, w = z.shape
    nh = dt_bias.shape[0]
    hd = w // nh
    ng = SSD_GROUPS
    hpg = nh // ng
    gw = hpg * hd
    ns = (xa.shape[1] - w) // (2 * ng)
    assert ns == LANES and nh <= LANES and LANES % hd == 0 and hpg % (LANES // hd) == 0
    q = min(q, seqlen)
    nc = seqlen // q
    pad = lambda a: jnp.zeros((1, LANES), F32).at[0, :nh].set(a.astype(F32))
    hp = -(-hpg // SUBLANES) * SUBLANES
    gps = 2 if ng % 2 == 0 else 1
    assert ng * hp <= LANES and (w // ns) % gps == 0
    e_np = np.zeros((ng, 2 * LANES, gw), np.float32)
    sel_np = np.zeros((ng, LANES, LANES), np.float32)
    selpad_np = np.zeros((LANES, LANES), np.float32)
    for g in range(ng):
        for r in range(hpg):
            e_np[g, g * hpg + r, r * hd:(r + 1) * hd] = 1.0
            e_np[g, LANES + g * hpg + r, r * hd:(r + 1) * hd] = 1.0
            sel_np[g, g * hpg + r, r] = 1.0
            selpad_np[g * hpg + r, g * hp + r] = 1.0
    tri = np.tril(np.ones((q, q), np.float32))
    rowblk = lambda wd: pl.BlockSpec((q, wd), lambda i: (i, 0))
    one = pl.BlockSpec((1, LANES), lambda i: (0, 0))
    dtc, dtend, dout, cumg, cumt = pl.pallas_call(
        _ssd_decay_kernel,
        out_shape=[jax.ShapeDtypeStruct((t, LANES), F32)] * 3
        + [jax.ShapeDtypeStruct((t, ng * LANES), F32), jax.ShapeDtypeStruct((t // q, LANES, q), F32)],
        grid=(t // q,),
        in_specs=[rowblk(LANES), one, one, pl.BlockSpec((q, q), lambda i: (0, 0)),
                  pl.BlockSpec((ng, LANES, LANES), lambda i: (0, 0, 0)),
                  pl.BlockSpec((LANES, LANES), lambda i: (0, 0))],
        out_specs=[rowblk(LANES)] * 3 + [rowblk(ng * LANES),
                                         pl.BlockSpec((None, LANES, q), lambda i: (i, 0, 0))],
        compiler_params=_cparams(("parallel",)),
        name="ssd_decay",
    )(dt_raw, pad(dt_bias), pad(a_log), jnp.asarray(tri), jnp.asarray(sel_np), jnp.asarray(selpad_np))
    r3 = lambda a: a.reshape(bsz, seqlen, a.shape[1])
    xa3 = r3(xa)
    blk = lambda wd, off: pl.BlockSpec((bsz, q, gps * wd), lambda g, c: (0, c, off // gps + g))
    hblk = pl.BlockSpec((bsz, q, LANES), lambda g, c: (0, c, 0))
    out = pl.pallas_call(
        functools.partial(_ssd_kernel, hpg=hpg, hd=hd, hp=hp),
        out_shape=jax.ShapeDtypeStruct((bsz, seqlen, mix_width), BF16),
        grid=(ng // gps, nc),
        in_specs=[blk(gw, 0), blk(ns, w // ns), blk(ns, w // ns + ng), blk(gw, 0),
                  hblk, hblk, hblk, blk(LANES, 0),
                  pl.BlockSpec((bsz, None, gps * hp, q), lambda g, c: (0, c, g, 0)),
                  pl.BlockSpec((gps, 2 * LANES, gw), lambda g, c: (g, 0, 0)),
                  pl.BlockSpec((1, gps * gw), lambda g, c: (0, g)),
                  pl.BlockSpec((1, gps * gw), lambda g, c: (0, g))],
        out_specs=blk(gw, 0),
        scratch_shapes=[pltpu.VMEM((gps * bsz, ns, gw), F32)],
        compiler_params=_cparams(("parallel", "arbitrary")),
        name="ssd_mix",
    )(xa3, xa3, xa3, r3(z), r3(dtc), r3(dtend), r3(dout), r3(cumg),
      cumt.reshape(bsz, nc, LANES, q), jnp.asarray(e_np, BF16),
      jnp.repeat(d.astype(F32), hd).reshape(1, w), norm_w.astype(F32).reshape(1, w))
    return out.reshape(t, mix_width)


def _logf_cumsum_kernel(f_ref, b_ref, tri_ref, o_ref, car_ref):
    c = pl.program_id(1)

    @pl.when(c == 0)
    def _():
        car_ref[...] = jnp.zeros_like(car_ref)

    lf = -_softplus(-(f_ref[...] + b_ref[...]))
    cum = jnp.dot(tri_ref[...], lf, preferred_element_type=F32,
                  precision=lax.Precision.HIGHEST) + car_ref[...]
    car_ref[...] = cum[cum.shape[0] - 1:, :]
    c2 = cum * (-LOG2E)
    hi = c2.astype(BF16)
    r1 = c2 - hi.astype(F32)
    mid = r1.astype(BF16)
    lo = (r1 - mid.astype(F32)).astype(BF16)
    o_ref[...] = jnp.concatenate([hi, mid, lo], axis=1)


def logf_cumsum(f_raw, b_f, bsz, seqlen, tc=256):
    t = f_raw.shape[0]
    nh = b_f.shape[0]
    tc = min(tc, seqlen)
    nc = seqlen // tc
    b_pad = jnp.zeros((1, LANES), F32).at[0, :nh].set(b_f.astype(F32))
    tri = np.tril(np.ones((tc, tc), np.float32))
    return pl.pallas_call(
        _logf_cumsum_kernel,
        out_shape=jax.ShapeDtypeStruct((t, 3 * LANES), BF16),
        grid=(bsz, nc),
        in_specs=[pl.BlockSpec((tc, LANES), lambda b, c: (b * nc + c, 0)),
                  pl.BlockSpec((1, LANES), lambda b, c: (0, 0)),
                  pl.BlockSpec((tc, tc), lambda b, c: (0, 0))],
        out_specs=pl.BlockSpec((tc, 3 * LANES), lambda b, c: (b * nc + c, 0)),
        scratch_shapes=[pltpu.VMEM((1, LANES), F32)],
        compiler_params=_cparams(("parallel", "arbitrary")),
        name="logf_cumsum",
    )(f_raw, b_pad, jnp.asarray(tri))


def _fox_kernel(q_ref, k_ref, v_ref, cum_ref, gate_ref, o_ref, qa_ref, ka_ref, va_ref, s_ref, p_ref,
                m_ref, al_ref, acc_ref, *, scale, tb):
    h = pl.program_id(1)
    qi = pl.program_id(2)
    hd = q_ref.shape[1]

    @pl.when(qi == 0)
    def _():
        nk = k_ref.shape[0]
        r = lax.broadcasted_iota(jnp.int32, (3 * LANES, LANES), 0)
        c = lax.broadcasted_iota(jnp.int32, (3 * LANES, LANES), 1)
        place = jnp.where((r == c * LANES + h) & (c < 3), 1.0, 0.0).astype(BF16)
        ka_ref[:, :hd] = k_ref[...]
        ka_ref[:, hd:] = jnp.dot(cum_ref[...], place, preferred_element_type=F32).astype(BF16)
        va_ref[:, :hd] = v_ref[...]
        va_ref[:, hd:] = jnp.ones((nk, LANES), BF16)

    qa_ref[:, :hd] = (q_ref[...].astype(F32) * (scale * LOG2E)).astype(BF16)
    qa_ref[:, hd:] = jnp.where(lax.broadcasted_iota(jnp.int32, (tb, LANES), 1) < 3, 1.0, 0.0).astype(BF16)

    half = tb // 2

    def pieces(diag):
        return [(0, half, half), (half, tb, tb)] if diag else [(0, tb, tb)]

    def qk_stage(j, diag):
        for r0, r1, nk in pieces(diag):
            s_ref[j % 2, r0:r1, :nk] = lax.dot_general(
                qa_ref[r0:r1, :], ka_ref[j * tb:j * tb + nk, :], (((1,), (1,)), ((), ())),
                preferred_element_type=F32)

    def softmax_stage(j, diag):
        for r0, r1, nk in pieces(diag):
            t = s_ref[j % 2, r0:r1, :nk]
            if diag:
                keep = (lax.broadcasted_iota(jnp.int32, (r1 - r0, nk), 0) + r0
                        >= lax.broadcasted_iota(jnp.int32, (r1 - r0, nk), 1))
                t = jnp.where(keep, t, -jnp.inf)
            m = m_ref[r0:r1, :]
            m_new = jnp.maximum(m, jnp.max(t, axis=1, keepdims=True))
            al_ref[r0:r1, :] = jnp.exp2(m - m_new)
            m_ref[r0:r1, :] = m_new
            p_ref[j % 2, r0:r1, :nk] = jnp.exp2(t - jnp.concatenate([m_new] * (nk // LANES), axis=1)
                                                ).astype(BF16)

    def pv_stage(j, diag):
        for r0, r1, nk in pieces(diag):
            alpha = al_ref[r0:r1, :]
            acc_ref[r0:r1, :] = (jnp.concatenate([alpha, alpha], axis=1) * acc_ref[r0:r1, :]
                                 + jnp.dot(p_ref[j % 2, r0:r1, :nk], va_ref[j * tb:j * tb + nk, :],
                                           preferred_element_type=F32))

    for nkb in range(1, ka_ref.shape[0] // tb + 1):
        @pl.when(qi == nkb - 1)
        def _(nkb=nkb):
            last = nkb - 1
            m_ref[...] = jnp.full_like(m_ref, -jnp.inf)
            acc_ref[...] = jnp.zeros_like(acc_ref)
            qk_stage(0, last == 0)
            for j in range(nkb):
                if j > 0:
                    pv_stage(j - 1, False)
                softmax_stage(j, j == last)
                if j + 1 < nkb:
                    qk_stage(j + 1, j + 1 == last)
            pv_stage(last, True)
            acc = acc_ref[...]
            att = acc[:, :hd] / acc[:, hd:]
            o_ref[...] = (att * _silu(gate_ref[...].astype(F32))).astype(o_ref.dtype)


def fox_attention(qkvg, cum, nh, hd, bsz, seqlen, tb=1024):
    t = qkvg.shape[0]
    tb = min(tb, seqlen)
    nb = seqlen // tb
    assert hd == LANES
    return pl.pallas_call(
        functools.partial(_fox_kernel, scale=1.0 / math.sqrt(hd), tb=tb),
        out_shape=jax.ShapeDtypeStruct((t, nh * hd), BF16),
        grid=(bsz, nh, nb),
        in_specs=[pl.BlockSpec((tb, hd), lambda b, h, i: (b * nb + i, h)),
                  pl.BlockSpec((seqlen, hd), lambda b, h, i: (b, nh + h)),
                  pl.BlockSpec((seqlen, hd), lambda b, h, i: (b, 2 * nh + h)),
                  pl.BlockSpec((seqlen, 3 * LANES), lambda b, h, i: (b, 0)),
                  pl.BlockSpec((tb, hd), lambda b, h, i: (b * nb + i, 3 * nh + h))],
        out_specs=pl.BlockSpec((tb, hd), lambda b, h, i: (b * nb + i, h)),
        scratch_shapes=[pltpu.VMEM((tb, 2 * hd), BF16), pltpu.VMEM((seqlen, 2 * hd), BF16),
                        pltpu.VMEM((seqlen, 2 * hd), BF16), pltpu.VMEM((2, tb, tb), F32),
                        pltpu.VMEM((2, tb, tb), BF16), pltpu.VMEM((tb, LANES), F32),
                        pltpu.VMEM((tb, LANES), F32), pltpu.VMEM((tb, 2 * hd), F32)],
        compiler_params=_cparams(("parallel", "parallel", "arbitrary")),
        name="fox_attention",
    )(qkvg, qkvg, qkvg, cum, qkvg)


def _pad_cols(w, n):
    return jnp.pad(w, ((0, 0), (0, n - w.shape[1])))


def kernel(x, l0_norm_w, l0_w_in, l0_s5_lambda_re, l0_s5_lambda_im, l0_s5_log_step, l0_s5_b_re, l0_s5_b_im, l0_s5_c_re, l0_s5_c_im, l0_s5_d, l0_s5_w_glu, l0_s5_b_glu, l0_ssd_conv_w, l0_ssd_conv_b, l0_ssd_dt_bias, l0_ssd_a_log, l0_ssd_d, l0_ssd_norm_w, l0_w_out, l1_norm_w, l1_w_in, l1_fox_b_f, l1_w_out, final_norm_w):
    bsz, seqlen, dm = x.shape
    t = bsz * seqlen
    x2 = x.reshape(t, dm)

    s5_w = l0_s5_w_glu.shape[0]
    ssd_w = l0_ssd_norm_w.shape[0]
    xbc_w = l0_ssd_conv_w.shape[1]
    ssd_h = l0_ssd_dt_bias.shape[0]
    mix_w = s5_w + ssd_w
    o_z = 2 * s5_w
    o_xbc = o_z + ssd_w
    o_dt = o_xbc + xbc_w

    h0 = rmsnorm(x2, l0_norm_w, BF16)
    w0 = l0_w_in.astype(BF16)
    ug = matmul(h0, w0, BF16, n=o_z)
    z = matmul(h0, w0, BF16, n=ssd_w, col0=o_z)
    xbc = matmul(h0, w0, BF16, n=xbc_w, col0=o_xbc)
    dt_raw = matmul(h0, _pad_cols(w0[:, o_dt:o_dt + ssd_h], LANES), F32)

    gact = s5_scan(ug, l0_s5_lambda_re, l0_s5_lambda_im, l0_s5_log_step, l0_s5_b_re, l0_s5_b_im,
                   l0_s5_c_re, l0_s5_c_im, l0_s5_d, bsz, seqlen)
    xa = ssd_conv(xbc, l0_ssd_conv_w, l0_ssd_conv_b, bsz, seqlen)
    mixed = ssd_mix(xa, z, dt_raw, l0_ssd_dt_bias, l0_ssd_a_log, l0_ssd_d, l0_ssd_norm_w,
                    mix_w, bsz, seqlen)
    mixed = s5_glu(gact, l0_s5_w_glu, l0_s5_b_glu, ug, mixed, ssd_w)
    x1 = matmul(mixed, l0_w_out.astype(BF16), F32, res=x2, rot=s5_w, tm=512, tn=512)

    fox_w = l1_w_out.shape[0]
    nh = l1_fox_b_f.shape[0]
    hd = fox_w // nh
    h1 = rmsnorm(x1, l1_norm_w, BF16)
    w1 = l1_w_in.astype(BF16)
    qkvg = matmul(h1, w1, BF16, n=4 * fox_w)
    f_raw = matmul(h1, _pad_cols(w1[:, 4 * fox_w:4 * fox_w + nh], LANES), F32)
    cum = logf_cumsum(f_raw, l1_fox_b_f, bsz, seqlen)
    att = fox_attention(qkvg, cum, nh, hd, bsz, seqlen)
    x2_out = matmul(att, l1_w_out.astype(BF16), F32, res=x1, tn=512)

    return rmsnorm(x2_out, final_norm_w, F32).reshape(bsz, seqlen, dm)
```

```python
import functools
import math

import numpy as np
import jax
import jax.numpy as jnp
from jax import lax
from jax.experimental import pallas as pl
from jax.experimental.pallas import tpu as pltpu

F32 = jnp.float32
BF16 = jnp.bfloat16

NORM_EPS = 1e-5
S5_EIG_CLIP = -1e-4
SSD_GROUPS = 8
LANES = 128
SUBLANES = 8
VMEM_LIMIT = 56 * 1024 * 1024
LOG2E = 1.4426950408889634


def _cparams(sem):
    return pltpu.CompilerParams(dimension_semantics=sem, vmem_limit_bytes=VMEM_LIMIT)


def _pick(n, pref):
    if n <= pref:
        return n
    t = pref
    while t >= LANES:
        if n % t == 0:
            return t
        t -= LANES
    return n


def _sigmoid(x):
    return 1.0 / (1.0 + jnp.exp(-x))


def _silu(x):
    return x * _sigmoid(x)


def _softplus(x):
    return jnp.maximum(x, 0.0) + jnp.log1p(jnp.exp(-jnp.abs(x)))


def _rmsnorm_kernel(x_ref, w_ref, o_ref):
    x = x_ref[...]
    ms = jnp.mean(x * x, axis=-1, keepdims=True)
    o_ref[...] = ((x * lax.rsqrt(ms + NORM_EPS)) * w_ref[...]).astype(o_ref.dtype)


def rmsnorm(x2d, w, out_dtype):
    t, d = x2d.shape
    tm = _pick(t, 256)
    return pl.pallas_call(
        _rmsnorm_kernel,
        out_shape=jax.ShapeDtypeStruct((t, d), out_dtype),
        grid=(t // tm,),
        in_specs=[pl.BlockSpec((tm, d), lambda i: (i, 0)),
                  pl.BlockSpec((1, d), lambda i: (0, 0))],
        out_specs=pl.BlockSpec((tm, d), lambda i: (i, 0)),
        compiler_params=_cparams(("parallel",)),
        name="rmsnorm",
    )(x2d, w.reshape(1, d).astype(F32))


def _mm_kernel(a_ref, b_ref, *rest, has_res, rot):
    if has_res:
        r_ref, o_ref = rest
    else:
        (o_ref,) = rest
    kd = a_ref.shape[1]
    if rot:
        acc = (jnp.dot(a_ref[:, :kd - rot], b_ref[rot:, :], preferred_element_type=F32)
               + jnp.dot(a_ref[:, kd - rot:], b_ref[:rot, :], preferred_element_type=F32))
    else:
        acc = jnp.dot(a_ref[...], b_ref[...], preferred_element_type=F32)
    if has_res:
        acc = r_ref[...] + acc
    o_ref[...] = acc.astype(o_ref.dtype)


def matmul(a, b, out_dtype, n=None, col0=0, res=None, rot=0, tm=1024, tn=1024):
    m, kd = a.shape
    n = b.shape[1] if n is None else n
    tm, tn = _pick(m, tm), _pick(n, tn)
    while col0 % tn or n % tn:
        tn -= LANES
    cb = col0 // tn
    in_specs = [pl.BlockSpec((tm, kd), lambda i, j: (i, 0)),
                pl.BlockSpec((kd, tn), lambda i, j: (0, cb + j))]
    args = [a, b]
    if res is not None:
        in_specs.append(pl.BlockSpec((tm, tn), lambda i, j: (i, j)))
        args.append(res)
    return pl.pallas_call(
        functools.partial(_mm_kernel, has_res=res is not None, rot=rot),
        out_shape=jax.ShapeDtypeStruct((m, n), out_dtype),
        grid=(m // tm, n // tn),
        in_specs=in_specs,
        out_specs=pl.BlockSpec((tm, tn), lambda i, j: (i, j)),
        compiler_params=_cparams(("parallel", "parallel")),
        name="matmul",
    )(*args)


def _mm_conv_kernel(a_ref, b_ref, w_ref, cb_ref, o_ref, tail_ref, *, seqlen, nsub):
    i = pl.program_id(0)
    j = pl.program_id(1)
    tm = a_ref.shape[0]
    kw = w_ref.shape[0]
    prev = jnp.where((i * tm) % seqlen == 0, 0.0, tail_ref[j])
    rs = tm // nsub
    for r in range(nsub):
        acc = jnp.dot(a_ref[r * rs:(r + 1) * rs, :], b_ref[...], preferred_element_type=F32)
        ext = jnp.concatenate([prev, acc], axis=0)
        prev = acc[rs - SUBLANES:, :]
        y = cb_ref[...] + w_ref[kw - 1:kw, :] * acc
        for k in range(kw - 1):
            y = y + w_ref[k:k + 1, :] * pltpu.roll(ext, kw - 1 - k, axis=0)[SUBLANES:, :]
        o_ref[r * rs:(r + 1) * rs, :] = _silu(y).astype(o_ref.dtype)
    tail_ref[j] = prev


def matmul_conv(a, b, conv_w, conv_b, n, col0, seqlen, tm=1024, tn=1024, nsub=2):
    m, kd = a.shape
    tm, tn = _pick(min(m, seqlen), tm), _pick(n, tn)
    while col0 % tn or n % tn:
        tn -= LANES
    assert seqlen % tm == 0 and conv_w.shape[0] - 1 <= SUBLANES
    cb = col0 // tn
    kw = conv_w.shape[0]
    return pl.pallas_call(
        functools.partial(_mm_conv_kernel, seqlen=seqlen, nsub=nsub),
        out_shape=jax.ShapeDtypeStruct((m, n), BF16),
        grid=(m // tm, n // tn),
        in_specs=[pl.BlockSpec((tm, kd), lambda i, j: (i, 0)),
                  pl.BlockSpec((kd, tn), lambda i, j: (0, cb + j)),
                  pl.BlockSpec((kw, tn), lambda i, j: (0, j)),
                  pl.BlockSpec((1, tn), lambda i, j: (0, j))],
        out_specs=pl.BlockSpec((tm, tn), lambda i, j: (i, j)),
        scratch_shapes=[pltpu.VMEM((n // tn, SUBLANES, tn), F32)],
        compiler_params=_cparams(("arbitrary", "arbitrary")),
        name="matmul_conv",
    )(a, b, conv_w.astype(F32), conv_b.astype(F32).reshape(1, n))


def _s5_prep_kernel(lr_ref, li_ref, st_ref, br_ref, bi_ref, lrf_ref, lif_ref, stf_ref,
                    bbr_ref, bbi_ref, pre_ref, pim_ref):
    lr = jnp.minimum(lr_ref[...], S5_EIG_CLIP)
    li = li_ref[...]
    step = jnp.exp(st_ref[...])
    mag = jnp.exp(lr * step)
    ab_re = mag * jnp.cos(li * step)
    ab_im = mag * jnp.sin(li * step)
    denom = lr * lr + li * li
    nr = ab_re - 1.0
    ni = ab_im
    coef_re = (nr * lr + ni * li) / denom
    coef_im = (ni * lr - nr * li) / denom
    br = br_ref[...]
    bi = bi_ref[...]
    bbr_ref[...] = coef_re * br - coef_im * bi
    bbi_ref[...] = coef_re * bi + coef_im * br
    lrf = jnp.minimum(lrf_ref[...], S5_EIG_CLIP)
    stepf = jnp.exp(stf_ref[...])
    n = pre_ref.shape[0]
    kk = (lax.broadcasted_iota(jnp.int32, (n, 1), 0) + 1).astype(F32)
    magp = jnp.exp(kk * (lrf * stepf))
    ang = kk * (lif_ref[...] * stepf)
    pre_ref[...] = magp * jnp.cos(ang)
    pim_ref[...] = magp * jnp.sin(ang)


def _s5_scan_kernel(u_ref, pm_ref, pmt_ref, bre_ref, bim_ref, cre_ref, cim_ref, d_ref,
                    pre_ref, pim_ref, o_ref, car_re, car_im, s_re, s_im, sb_re, sb_im, *, n):
    c = pl.program_id(1)
    nb = u_ref.shape[0]
    tps = bre_ref.shape[0]
    w = bre_ref.shape[2]
    chains = [(tt, b) for tt in range(tps) for b in range(nb)]
    ids = range(len(chains))
    lanes = lambda tt: slice(tt * LANES, (tt + 1) * LANES)
    states = lambda tt: slice(tt * w, (tt + 1) * w)

    @pl.when(c == 0)
    def _():
        car_re[...] = jnp.zeros_like(car_re)
        car_im[...] = jnp.zeros_like(car_im)

    sl = lambda i: pl.ds(i * SUBLANES, SUBLANES)
    sl2 = lambda i: pl.ds(i * 2 * SUBLANES, 2 * SUBLANES)
    up = [jnp.dot(pm_ref[...], u_ref[b, :, lanes(tt)], preferred_element_type=F32)
          for tt, b in chains]
    for k, (tt, b) in enumerate(chains):
        upb = up[k].astype(BF16)
        s_re[k] = jnp.dot(upb, bre_ref[tt], preferred_element_type=F32)
        s_im[k] = jnp.dot(upb, bim_ref[tt], preferred_element_type=F32)
    a_re = [jnp.broadcast_to(pre_ref[0:1, states(tt)], (SUBLANES, w)) for tt in range(tps)]
    a_im = [jnp.broadcast_to(pim_ref[0:1, states(tt)], (SUBLANES, w)) for tt in range(tps)]
    x_re = [jnp.zeros((SUBLANES, w), F32) for _ in ids]
    x_im = [jnp.zeros((SUBLANES, w), F32) for _ in ids]
    for i in range(n):
        for k, (tt, b) in enumerate(chains):
            n_re = a_re[tt] * x_re[k] - a_im[tt] * x_im[k] + s_re[k, sl(i), :]
            n_im = a_re[tt] * x_im[k] + a_im[tt] * x_re[k] + s_im[k, sl(i), :]
            x_re[k], x_im[k] = n_re, n_im
            s_re[k, sl(i), :] = n_re
            s_im[k, sl(i), :] = n_im
    cm_re, cm_im = [], []
    for k, (tt, b) in enumerate(chains):
        an_re = pre_ref[n - 1:n, states(tt)]
        an_im = pim_ref[n - 1:n, states(tt)]
        c_re = car_re[k]
        c_im = car_im[k]
        rows_re, rows_im = [c_re], [c_im]
        for s in range(1, SUBLANES + 1):
            e_re = x_re[k][s - 1:s, :]
            e_im = x_im[k][s - 1:s, :]
            c_re, c_im = (e_re + an_re * c_re - an_im * c_im,
                          e_im + an_re * c_im + an_im * c_re)
            if s < SUBLANES:
                rows_re.append(c_re)
                rows_im.append(c_im)
        car_re[k] = c_re
        car_im[k] = c_im
        cm_re.append(jnp.concatenate(rows_re + rows_re, axis=0))
        cm_im.append(jnp.concatenate(rows_im + rows_im, axis=0))
    for i in range(n // 2):
        p_re = [jnp.concatenate([jnp.broadcast_to(pre_ref[2 * i + r:2 * i + r + 1, states(tt)], (SUBLANES, w))
                                 for r in range(2)], axis=0) for tt in range(tps)]
        p_im = [jnp.concatenate([jnp.broadcast_to(pim_ref[2 * i + r:2 * i + r + 1, states(tt)], (SUBLANES, w))
                                 for r in range(2)], axis=0) for tt in range(tps)]
        for k, (tt, b) in enumerate(chains):
            sb_re[k, sl2(i), :] = (s_re[k, sl2(i), :] + p_re[tt] * cm_re[k] - p_im[tt] * cm_im[k]).astype(BF16)
            sb_im[k, sl2(i), :] = (s_im[k, sl2(i), :] + p_re[tt] * cm_im[k] + p_im[tt] * cm_re[k]).astype(BF16)
    for k, (tt, b) in enumerate(chains):
        yp = (jnp.dot(sb_re[k], cre_ref[tt], preferred_element_type=F32)
              - jnp.dot(sb_im[k], cim_ref[tt], preferred_element_type=F32)
              + d_ref[tt] * up[k])
        y = jnp.dot(pmt_ref[...], yp.astype(BF16), preferred_element_type=F32)
        cdf = 0.5 * (1.0 + jnp.tanh(np.float32(math.sqrt(2.0 / math.pi)) * (y + 0.044715 * (y * y * y))))
        o_ref[b, :, lanes(tt)] = (y * cdf).astype(o_ref.dtype)


def _s5_perm(tc):
    n = tc // SUBLANES
    pm = np.zeros((tc, tc), np.float32)
    for i in range(n):
        for s in range(SUBLANES):
            pm[i * SUBLANES + s, s * n + i] = 1.0
    return pm


def s5_scan(ug, lam_re, lam_im, log_step, b_re, b_im, c_re, c_im, d, bsz, seqlen, tc=256):
    g, p = lam_re.shape
    hh = b_re.shape[2]
    width = g * hh
    gpt = LANES // hh
    nt = width // LANES
    sw = gpt * p
    tc = min(tc, seqlen)
    n = tc // SUBLANES
    nc = seqlen // tc
    t = bsz * seqlen

    rep = lambda a: jnp.repeat(a.astype(F32), hh, axis=1)
    lr_rep, li_rep = rep(lam_re), rep(lam_im)
    st_rep = jnp.broadcast_to(log_step.astype(F32)[:, None], (g, p * hh))
    flat = lambda a: a.astype(F32).reshape(1, g * p)
    st_flat = jnp.broadcast_to(log_step.astype(F32)[:, None], (g, p)).reshape(1, g * p)
    bb_re, bb_im, ptab_re, ptab_im = pl.pallas_call(
        _s5_prep_kernel,
        out_shape=[jax.ShapeDtypeStruct((g, p * hh), F32)] * 2
        + [jax.ShapeDtypeStruct((n, g * p), F32)] * 2,
        name="s5_prep",
    )(lr_rep, li_rep, st_rep, b_re.astype(F32).reshape(g, p * hh), b_im.astype(F32).reshape(g, p * hh),
      flat(lam_re), flat(lam_im), st_flat)

    eye = jnp.eye(gpt, dtype=F32)
    bd_b = lambda bb: jnp.einsum('jgph,gk->jghkp', bb.reshape(nt, gpt, p, hh), eye
                                 ).reshape(nt, LANES, sw).astype(BF16)
    bd_c = lambda cc: jnp.einsum('jghp,gk->jkpgh', cc.astype(F32).reshape(nt, gpt, hh, p), eye
                                 ).reshape(nt, sw, LANES).astype(BF16)
    pm = _s5_perm(tc)
    tps = 2 if nt % 2 == 0 else 1
    nch = tps * bsz
    tile3 = lambda r, c: pl.BlockSpec((tps, r, c), lambda j, k: (j, 0, 0))
    full = lambda r, c: pl.BlockSpec((r, c), lambda j, k: (0, 0))
    out = pl.pallas_call(
        functools.partial(_s5_scan_kernel, n=n),
        out_shape=jax.ShapeDtypeStruct((bsz, seqlen, width), BF16),
        grid=(nt // tps, nc),
        in_specs=[pl.BlockSpec((bsz, tc, tps * LANES), lambda j, k: (0, k, j)),
                  full(tc, tc), full(tc, tc),
                  tile3(LANES, sw), tile3(LANES, sw), tile3(sw, LANES), tile3(sw, LANES),
                  tile3(1, LANES),
                  pl.BlockSpec((n, tps * sw), lambda j, k: (0, j)),
                  pl.BlockSpec((n, tps * sw), lambda j, k: (0, j))],
        out_specs=pl.BlockSpec((bsz, tc, tps * LANES), lambda j, k: (0, k, j)),
        scratch_shapes=[pltpu.VMEM((nch, 1, sw), F32), pltpu.VMEM((nch, 1, sw), F32),
                        pltpu.VMEM((nch, tc, sw), F32), pltpu.VMEM((nch, tc, sw), F32),
                        pltpu.VMEM((nch, tc, sw), BF16), pltpu.VMEM((nch, tc, sw), BF16)],
        compiler_params=_cparams(("parallel", "arbitrary")),
        name="s5_scan",
    )(ug.reshape(bsz, seqlen, ug.shape[1]), jnp.asarray(pm, BF16), jnp.asarray(pm.T, BF16),
      bd_b(bb_re), bd_b(bb_im), bd_c(c_re), bd_c(c_im), d.astype(F32).reshape(nt, 1, LANES),
      ptab_re, ptab_im)
    return out.reshape(t, width)


def _glu_kernel(a_ref, w_ref, b_ref, gn_ref, gate_ref, mixed_ref, o_ref):
    del mixed_ref
    z = jnp.dot(a_ref[...], w_ref[...], preferred_element_type=F32) + b_ref[...]
    gn = gn_ref[...].astype(F32)
    o_ref[...] = ((gn * _sigmoid(z)) * _silu(gate_ref[...].astype(F32))).astype(o_ref.dtype)


def s5_glu(gact, w_glu, b_glu, ug, mixed, col0, tm=512, tn=512):
    t, width = gact.shape
    tm, tn = _pick(t, tm), _pick(width, tn)
    while col0 % tn:
        tn -= LANES
    nj = width // tn
    return pl.pallas_call(
        _glu_kernel,
        out_shape=jax.ShapeDtypeStruct(mixed.shape, mixed.dtype),
        grid=(t // tm, nj),
        in_specs=[pl.BlockSpec((tm, width), lambda i, j: (i, 0)),
                  pl.BlockSpec((width, tn), lambda i, j: (0, j)),
                  pl.BlockSpec((1, tn), lambda i, j: (0, j)),
                  pl.BlockSpec((tm, tn), lambda i, j: (i, j)),
                  pl.BlockSpec((tm, tn), lambda i, j: (i, nj + j)),
                  pl.BlockSpec(memory_space=pl.ANY)],
        out_specs=pl.BlockSpec((tm, tn), lambda i, j: (i, col0 // tn + j)),
        input_output_aliases={5: 0},
        compiler_params=_cparams(("parallel", "arbitrary")),
        name="s5_glu",
    )(gact, w_glu.astype(BF16), b_glu.astype(F32).reshape(1, width), gact, ug, mixed)


def _conv_kernel(cur_ref, tail_ref, w_ref, b_ref, o_ref, *, halo):
    c = pl.program_id(1)
    cur = cur_ref[...].astype(F32)
    tail = jnp.where(c > 0, tail_ref[...].astype(F32), 0.0)
    ext = jnp.concatenate([tail, cur], axis=0)
    kw = w_ref.shape[0]
    acc = b_ref[...] + w_ref[kw - 1:kw, :] * cur
    for k in range(kw - 1):
        acc = acc + w_ref[k:k + 1, :] * pltpu.roll(ext, kw - 1 - k, axis=0)[halo:, :]
    o_ref[...] = _silu(acc).astype(o_ref.dtype)


def ssd_conv(xbc, conv_w, conv_b, bsz, seqlen, tc=512, wc=1024):
    t, ch = xbc.shape
    tc, wc = _pick(seqlen, tc), _pick(ch, wc)
    nc = seqlen // tc
    halo = 16
    hb = tc // halo
    return pl.pallas_call(
        functools.partial(_conv_kernel, halo=halo),
        out_shape=jax.ShapeDtypeStruct((t, ch), BF16),
        grid=(bsz, nc, ch // wc),
        in_specs=[pl.BlockSpec((tc, wc), lambda b, c, w: (b * nc + c, w)),
                  pl.BlockSpec((halo, wc), lambda b, c, w: (jnp.maximum((b * nc + c) * hb - 1, 0), w)),
                  pl.BlockSpec((conv_w.shape[0], wc), lambda b, c, w: (0, w)),
                  pl.BlockSpec((1, wc), lambda b, c, w: (0, w))],
        out_specs=pl.BlockSpec((tc, wc), lambda b, c, w: (b * nc + c, w)),
        compiler_params=_cparams(("parallel", "parallel", "parallel")),
        name="ssd_conv",
    )(xbc, xbc, conv_w.astype(F32), conv_b.astype(F32).reshape(1, ch))


def _ssd_decay_kernel(dt_ref, dtb_ref, alog_ref, tri_ref, sel_ref, selpad_ref,
                      dt_o, dtend_o, dout_o, cumg_o, cumt_o):
    q = dt_ref.shape[0]
    dt = _softplus(dt_ref[...] + dtb_ref[...])
    la = dt * (-jnp.exp(alog_ref[...]))
    cum = jnp.dot(tri_ref[...], la, preferred_element_type=F32, precision=lax.Precision.HIGHEST)
    dt_o[...] = dt
    dtend_o[...] = dt * jnp.exp(cum[q - 1:q, :] - cum)
    dout_o[...] = jnp.exp(cum)
    for g in range(sel_ref.shape[0]):
        cumg_o[:, g * LANES:(g + 1) * LANES] = jnp.dot(cum, sel_ref[g], preferred_element_type=F32,
                                                       precision=lax.Precision.HIGHEST)
    cumt_o[...] = jnp.dot(cum, selpad_ref[...], preferred_element_type=F32,
                          precision=lax.Precision.HIGHEST).T


def _ssd_kernel(x_ref, b_ref, c_ref, z_ref, dt_ref, dtend_ref, dout_ref, cumg_ref, cumt_ref, e_ref,
                d_ref, nw_ref, o_ref, st_ref, *, hpg, hd, hp):
    ck = pl.program_id(1)

    @pl.when(ck == 0)
    def _():
        st_ref[...] = jnp.zeros_like(st_ref)

    nb, q = x_ref.shape[0], x_ref.shape[1]
    gps = e_ref.shape[0]
    gw = e_ref.shape[2]
    ns = b_ref.shape[2] // gps
    chains = [(gg, b) for gg in range(gps) for b in range(nb)]
    ids = range(len(chains))
    chan = lambda gg: slice(gg * gw, (gg + 1) * gw)
    stat = lambda gg: slice(gg * ns, (gg + 1) * ns)
    causal = (lax.broadcasted_iota(jnp.int32, (q, q), 0) >= lax.broadcasted_iota(jnp.int32, (q, q), 1))
    lane = lax.broadcasted_iota(jnp.int32, (q, LANES), 1)
    hpt = LANES // hd

    vs = []
    for b in range(nb):
        dout = dout_ref[b]
        v = jnp.concatenate([dt_ref[b], dtend_ref[b], dout,
                             jnp.broadcast_to(dout[q - 1:q, :], (SUBLANES, LANES))], axis=0)
        hi = v.astype(BF16)
        vs.append(jnp.concatenate([hi, (v - hi.astype(F32)).astype(BF16)], axis=1))
    ex = [jnp.dot(vs[b], e_ref[gg], preferred_element_type=F32) for gg, b in chains]
    x = [x_ref[b, :, chan(gg)].astype(F32) for gg, b in chains]
    bm = [b_ref[b, :, stat(gg)] for gg, b in chains]
    cm = [c_ref[b, :, stat(gg)] for gg, b in chains]
    xdt_b = [(x[k] * ex[k][:q]).astype(BF16) for k in ids]
    xde = [(x[k] * ex[k][q:2 * q]).astype(BF16) for k in ids]
    scores = [lax.dot_general(cm[k], bm[k], (((1,), (1,)), ((), ())), preferred_element_type=F32)
              for k in ids]
    parts = [[] for _ in ids]
    for tl in range(hpg // hpt):
        acc = [None for _ in ids]
        for hh in range(hpt):
            r = tl * hpt + hh
            for k, (gg, b) in enumerate(chains):
                seg = (cumg_ref[b, :, gg * LANES + r:gg * LANES + r + 1]
                       - cumt_ref[b, gg * hp + r:gg * hp + r + 1, :])
                wgt = (jnp.where(causal, jnp.exp(seg), 0.0) * scores[k]).astype(BF16)
                xt = xdt_b[k][:, tl * LANES:(tl + 1) * LANES]
                xm = jnp.where((lane >= hh * hd) & (lane < (hh + 1) * hd), xt, jnp.zeros_like(xt))
                part = jnp.dot(wgt, xm, preferred_element_type=F32)
                acc[k] = part if acc[k] is None else acc[k] + part
        for k in ids:
            parts[k].append(acc[k])
    for k, (gg, b) in enumerate(chains):
        st = st_ref[k]
        y_off = jnp.dot(cm[k], st.astype(BF16), preferred_element_type=F32) * ex[k][2 * q:3 * q]
        upd = lax.dot_general(bm[k], xde[k], (((0,), (0,)), ((), ())), preferred_element_type=F32)
        st_ref[k] = st * ex[k][3 * q:3 * q + 1] + upd
        y = jnp.concatenate(parts[k], axis=1) + y_off + d_ref[:, chan(gg)] * x[k]
        y = y * _silu(z_ref[b, :, chan(gg)].astype(F32))
        ms = jnp.mean(y * y, axis=-1, keepdims=True)
        o_ref[b, :, chan(gg)] = ((y * lax.rsqrt(ms + NORM_EPS)) * nw_ref[:, chan(gg)]).astype(o_ref.dtype)


def ssd_mix(xa, z, dt_raw, dt_bias, a_log, d, norm_w, mix_width, bsz, seqlen, q=128):
    t, w = z.shape
    nh = dt_bias.shape[0]
    hd = w // nh
    ng = SSD_GROUPS
    hpg = nh // ng
    gw = hpg * hd
    ns = (xa.shape[1] - w) // (2 * ng)
    assert ns == LANES and nh <= LANES and LANES % hd == 0 and hpg % (LANES // hd) == 0
    q = min(q, seqlen)
    nc = seqlen // q
    pad = lambda a: jnp.zeros((1, LANES), F32).at[0, :nh].set(a.astype(F32))
    hp = -(-hpg // SUBLANES) * SUBLANES
    gps = 2 if ng % 2 == 0 else 1
    assert ng * hp <= LANES and (w // ns) % gps == 0
    e_np = np.zeros((ng, 2 * LANES, gw), np.float32)
    sel_np = np.zeros((ng, LANES, LANES), np.float32)
    selpad_np = np.zeros((LANES, LANES), np.float32)
    for g in range(ng):
        for r in range(hpg):
            e_np[g, g * hpg + r, r * hd:(r + 1) * hd] = 1.0
            e_np[g, LANES + g * hpg + r, r * hd:(r + 1) * hd] = 1.0
            sel_np[g, g * hpg + r, r] = 1.0
            selpad_np[g * hpg + r, g * hp + r] = 1.0
    tri = np.tril(np.ones((q, q), np.float32))
    rowblk = lambda wd: pl.BlockSpec((q, wd), lambda i: (i, 0))
    one = pl.BlockSpec((1, LANES), lambda i: (0, 0))
    dtc, dtend, dout, cumg, cumt = pl.pallas_call(
        _ssd_decay_kernel,
        out_shape=[jax.ShapeDtypeStruct((t, LANES), F32)] * 3
        + [jax.ShapeDtypeStruct((t, ng * LANES), F32), jax.ShapeDtypeStruct((t // q, LANES, q), F32)],
        grid=(t // q,),
        in_specs=[rowblk(LANES), one, one, pl.BlockSpec((q, q), lambda i: (0, 0)),
                  pl.BlockSpec((ng, LANES, LANES), lambda i: (0, 0, 0)),
                  pl.BlockSpec((LANES, LANES), lambda i: (0, 0))],
        out_specs=[rowblk(LANES)] * 3 + [rowblk(ng * LANES),
                                         pl.BlockSpec((None, LANES, q), lambda i: (i, 0, 0))],
        compiler_params=_cparams(("parallel",)),
        name="ssd_decay",
    )(dt_raw, pad(dt_bias), pad(a_log), jnp.asarray(tri), jnp.asarray(sel_np), jnp.asarray(selpad_np))
    r3 = lambda a: a.reshape(bsz, seqlen, a.shape[1])
    xa3 = r3(xa)
    blk = lambda wd, off: pl.BlockSpec((bsz, q, gps * wd), lambda g, c: (0, c, off // gps + g))
    hblk = pl.BlockSpec((bsz, q, LANES), lambda g, c: (0, c, 0))
    out = pl.pallas_call(
        functools.partial(_ssd_kernel, hpg=hpg, hd=hd, hp=hp),
        out_shape=jax.ShapeDtypeStruct((bsz, seqlen, mix_width), BF16),
        grid=(ng // gps, nc),
        in_specs=[blk(gw, 0), blk(ns, w // ns), blk(ns, w // ns + ng), blk(gw, 0),
                  hblk, hblk, hblk, blk(LANES, 0),
                  pl.BlockSpec((bsz, None, gps * hp, q), lambda g, c: (0, c, g, 0)),
                  pl.BlockSpec((gps, 2 * LANES, gw), lambda g, c: (g, 0, 0)),
                  pl.BlockSpec((1, gps * gw), lambda g, c: (0, g)),
                  pl.BlockSpec((1, gps * gw), lambda g, c: (0, g))],
        out_specs=blk(gw, 0),
        scratch_shapes=[pltpu.VMEM((gps * bsz, ns, gw), F32)],
        compiler_params=_cparams(("parallel", "arbitrary")),
        name="ssd_mix",
    )(xa3, xa3, xa3, r3(z), r3(dtc), r3(dtend), r3(dout), r3(cumg),
      cumt.reshape(bsz, nc, LANES, q), jnp.asarray(e_np, BF16),
      jnp.repeat(d.astype(F32), hd).reshape(1, w), norm_w.astype(F32).reshape(1, w))
    return out.reshape(t, mix_width)


def _logf_cumsum_kernel(f_ref, b_ref, tri_ref, o_ref, car_ref):
    c = pl.program_id(1)

    @pl.when(c == 0)
    def _():
        car_ref[...] = jnp.zeros_like(car_ref)

    lf = -_softplus(-(f_ref[...] + b_ref[...]))
    cum = jnp.dot(tri_ref[...], lf, preferred_element_type=F32,
                  precision=lax.Precision.HIGHEST) + car_ref[...]
    car_ref[...] = cum[cum.shape[0] - 1:, :]
    c2 = cum * (-LOG2E)
    hi = c2.astype(BF16)
    r1 = c2 - hi.astype(F32)
    mid = r1.astype(BF16)
    lo = (r1 - mid.astype(F32)).astype(BF16)
    o_ref[...] = jnp.concatenate([hi, mid, lo], axis=1)


def logf_cumsum(f_raw, b_f, bsz, seqlen, tc=256):
    t = f_raw.shape[0]
    nh = b_f.shape[0]
    tc = min(tc, seqlen)
    nc = seqlen // tc
    b_pad = jnp.zeros((1, LANES), F32).at[0, :nh].set(b_f.astype(F32))
    tri = np.tril(np.ones((tc, tc), np.float32))
    return pl.pallas_call(
        _logf_cumsum_kernel,
        out_shape=jax.ShapeDtypeStruct((t, 3 * LANES), BF16),
        grid=(bsz, nc),
        in_specs=[pl.BlockSpec((tc, LANES), lambda b, c: (b * nc + c, 0)),
                  pl.BlockSpec((1, LANES), lambda b, c: (0, 0)),
                  pl.BlockSpec((tc, tc), lambda b, c: (0, 0))],
        out_specs=pl.BlockSpec((tc, 3 * LANES), lambda b, c: (b * nc + c, 0)),
        scratch_shapes=[pltpu.VMEM((1, LANES), F32)],
        compiler_params=_cparams(("parallel", "arbitrary")),
        name="logf_cumsum",
    )(f_raw, b_pad, jnp.asarray(tri))


def _fox_kernel(q_ref, k_ref, v_ref, cum_ref, gate_ref, o_ref, qa_ref, ka_ref, va_ref, s_ref, p_ref,
                m_ref, al_ref, acc_ref, *, scale, tb):
    h = pl.program_id(1)
    hd = q_ref.shape[1]
    nk = k_ref.shape[0]
    nqb = nk // tb

    r = lax.broadcasted_iota(jnp.int32, (3 * LANES, LANES), 0)
    c = lax.broadcasted_iota(jnp.int32, (3 * LANES, LANES), 1)
    place = jnp.where((r == c * LANES + h) & (c < 3), 1.0, 0.0).astype(BF16)
    ka_ref[:, :hd] = k_ref[...]
    ka_ref[:, hd:] = jnp.dot(cum_ref[...], place, preferred_element_type=F32).astype(BF16)
    va_ref[:, :hd] = v_ref[...]
    va_ref[:, hd:] = jnp.ones((nk, LANES), BF16)
    qa_ref[:, :hd] = (q_ref[...].astype(F32) * (scale * LOG2E)).astype(BF16)
    qa_ref[:, hd:] = jnp.where(lax.broadcasted_iota(jnp.int32, (nk, LANES), 1) < 3, 1.0, 0.0).astype(BF16)
    m_ref[...] = jnp.full_like(m_ref, -jnp.inf)
    acc_ref[...] = jnp.zeros_like(acc_ref)

    half = tb // 2
    pairs = [(qi, j) for qi in range(nqb) for j in range(qi + 1)]

    def pieces(qi, j):
        q0, k0 = qi * tb, j * tb
        if qi == j:
            return [(q0, q0 + half, k0, half), (q0 + half, q0 + tb, k0, tb)]
        return [(q0, q0 + tb, k0, tb)]

    def qk_stage(n):
        qi, j = pairs[n]
        for r0, r1, k0, nkeys in pieces(qi, j):
            s_ref[n % 2, r0 - qi * tb:r1 - qi * tb, :nkeys] = lax.dot_general(
                qa_ref[r0:r1, :], ka_ref[k0:k0 + nkeys, :], (((1,), (1,)), ((), ())),
                preferred_element_type=F32)

    def softmax_stage(n):
        qi, j = pairs[n]
        for r0, r1, k0, nkeys in pieces(qi, j):
            l0, l1 = r0 - qi * tb, r1 - qi * tb
            t = s_ref[n % 2, l0:l1, :nkeys]
            if qi == j:
                keep = (lax.broadcasted_iota(jnp.int32, (l1 - l0, nkeys), 0) + l0
                        >= lax.broadcasted_iota(jnp.int32, (l1 - l0, nkeys), 1))
                t = jnp.where(keep, t, -jnp.inf)
            m = m_ref[r0:r1, :]
            m_new = jnp.maximum(m, jnp.max(t, axis=1, keepdims=True))
            al_ref[n % 2, l0:l1, :] = jnp.exp2(m - m_new)
            m_ref[r0:r1, :] = m_new
            p_ref[n % 2, l0:l1, :nkeys] = jnp.exp2(t - jnp.concatenate([m_new] * (nkeys // LANES), axis=1)
                                                   ).astype(BF16)

    def pv_stage(n):
        qi, j = pairs[n]
        for r0, r1, k0, nkeys in pieces(qi, j):
            l0, l1 = r0 - qi * tb, r1 - qi * tb
            alpha = al_ref[n % 2, l0:l1, :]
            acc_ref[r0:r1, :] = (jnp.concatenate([alpha, alpha], axis=1) * acc_ref[r0:r1, :]
                                 + jnp.dot(p_ref[n % 2, l0:l1, :nkeys], va_ref[k0:k0 + nkeys, :],
                                           preferred_element_type=F32))
        if qi == j:
            q0 = qi * tb
            acc = acc_ref[q0:q0 + tb, :]
            att = acc[:, :hd] / acc[:, hd:]
            o_ref[q0:q0 + tb, :] = (att * _silu(gate_ref[q0:q0 + tb, :].astype(F32))).astype(o_ref.dtype)

    qk_stage(0)
    for n in range(len(pairs)):
        if n > 0:
            pv_stage(n - 1)
        softmax_stage(n)
        if n + 1 < len(pairs):
            qk_stage(n + 1)
    pv_stage(len(pairs) - 1)


def fox_attention(qkvg, cum, nh, hd, bsz, seqlen, tb=1024):
    t = qkvg.shape[0]
    tb = min(tb, seqlen)
    nb = seqlen // tb
    assert hd == LANES
    return pl.pallas_call(
        functools.partial(_fox_kernel, scale=1.0 / math.sqrt(hd), tb=tb),
        out_shape=jax.ShapeDtypeStruct((t, nh * hd), BF16),
        grid=(bsz, nh),
        in_specs=[pl.BlockSpec((seqlen, hd), lambda b, h: (b, h)),
                  pl.BlockSpec((seqlen, hd), lambda b, h: (b, nh + h)),
                  pl.BlockSpec((seqlen, hd), lambda b, h: (b, 2 * nh + h)),
                  pl.BlockSpec((seqlen, 3 * LANES), lambda b, h: (b, 0)),
                  pl.BlockSpec((seqlen, hd), lambda b, h: (b, 3 * nh + h))],
        out_specs=pl.BlockSpec((seqlen, hd), lambda b, h: (b, h)),
        scratch_shapes=[pltpu.VMEM((seqlen, 2 * hd), BF16), pltpu.VMEM((seqlen, 2 * hd), BF16),
                        pltpu.VMEM((seqlen, 2 * hd), BF16), pltpu.VMEM((2, tb, tb), F32),
                        pltpu.VMEM((2, tb, tb), BF16), pltpu.VMEM((seqlen, LANES), F32),
                        pltpu.VMEM((2, tb, LANES), F32), pltpu.VMEM((seqlen, 2 * hd), F32)],
        compiler_params=_cparams(("parallel", "parallel")),
        name="fox_attention",
    )(qkvg, qkvg, qkvg, cum, qkvg)


def _pad_cols(w, n):
    return jnp.pad(w, ((0, 0), (0, n - w.shape[1])))


def kernel(x, l0_norm_w, l0_w_in, l0_s5_lambda_re, l0_s5_lambda_im, l0_s5_log_step, l0_s5_b_re, l0_s5_b_im, l0_s5_c_re, l0_s5_c_im, l0_s5_d, l0_s5_w_glu, l0_s5_b_glu, l0_ssd_conv_w, l0_ssd_conv_b, l0_ssd_dt_bias, l0_ssd_a_log, l0_ssd_d, l0_ssd_norm_w, l0_w_out, l1_norm_w, l1_w_in, l1_fox_b_f, l1_w_out, final_norm_w):
    bsz, seqlen, dm = x.shape
    t = bsz * seqlen
    x2 = x.reshape(t, dm)

    s5_w = l0_s5_w_glu.shape[0]
    ssd_w = l0_ssd_norm_w.shape[0]
    xbc_w = l0_ssd_conv_w.shape[1]
    ssd_h = l0_ssd_dt_bias.shape[0]
    mix_w = s5_w + ssd_w
    o_z = 2 * s5_w
    o_xbc = o_z + ssd_w
    o_dt = o_xbc + xbc_w

    h0 = rmsnorm(x2, l0_norm_w, BF16)
    w0 = l0_w_in.astype(BF16)
    ug = matmul(h0, w0, BF16, n=o_z)
    z = matmul(h0, w0, BF16, n=ssd_w, col0=o_z)
    xa = matmul_conv(h0, w0, l0_ssd_conv_w, l0_ssd_conv_b, xbc_w, o_xbc, seqlen)
    dt_raw = matmul(h0, _pad_cols(w0[:, o_dt:o_dt + ssd_h], LANES), F32)

    gact = s5_scan(ug, l0_s5_lambda_re, l0_s5_lambda_im, l0_s5_log_step, l0_s5_b_re, l0_s5_b_im,
                   l0_s5_c_re, l0_s5_c_im, l0_s5_d, bsz, seqlen)
    mixed = ssd_mix(xa, z, dt_raw, l0_ssd_dt_bias, l0_ssd_a_log, l0_ssd_d, l0_ssd_norm_w,
                    mix_w, bsz, seqlen)
    mixed = s5_glu(gact, l0_s5_w_glu, l0_s5_b_glu, ug, mixed, ssd_w)
    x1 = matmul(mixed, l0_w_out.astype(BF16), F32, res=x2, rot=s5_w, tm=512, tn=512)

    fox_w = l1_w_out.shape[0]
    nh = l1_fox_b_f.shape[0]
    hd = fox_w // nh
    h1 = rmsnorm(x1, l1_norm_w, BF16)
    w1 = l1_w_in.astype(BF16)
    qkvg = matmul(h1, w1, BF16, n=4 * fox_w)
    f_raw = matmul(h1, _pad_cols(w1[:, 4 * fox_w:4 * fox_w + nh], LANES), F32)
    cum = logf_cumsum(f_raw, l1_fox_b_f, bsz, seqlen)
    att = fox_attention(qkvg, cum, nh, hd, bsz, seqlen)
    x2_out = matmul(att, l1_w_out.astype(BF16), F32, res=x1, tn=512)

    return rmsnorm(x2_out, final_norm_w, F32).reshape(bsz, seqlen, dm)
```
